```python
import jax, jax.numpy as jnp
from jax import lax
import numpy as np

D_MODEL = 2048
BATCH = 4
SEQ = 2048
DEPTH = 1

GRID_W = 64
CTX_LEN = 256
EPS = 1e-6
A_HEADS = 8
A_DK = 128
A_DV = 128
A_W = A_HEADS * A_DV
A_CONV = 5
A_CHUNK = 64
B_HEADS = 8
B_KV = 2
B_HD = 128
B_W = B_HEADS * B_HD
Q_BLOCK = 128
ROPE_THETA = 10000.0
CONV_CH = 2 * A_HEADS * A_DK + A_W
IN_COLS = 2 * A_HEADS * A_DK + 2 * A_W + 4 * A_HEADS + 2 * B_W + 2 * B_KV * B_HD + 2 * D_MODEL

kernel_name = 'hybrid_gdn_gqa_dit_block'


def rms_norm(x, w):
    xf = x.astype(jnp.float32)
    xf = xf * lax.rsqrt(jnp.mean(xf * xf, axis=-1, keepdims=True) + EPS)
    return (xf * w.astype(jnp.float32)).astype(x.dtype)


def l2_norm(x):
    return x * lax.rsqrt(jnp.sum(x * x, axis=-1, keepdims=True) + EPS)


def modulate(x, norm_w, shift, scale):
    return rms_norm(x, norm_w) * (1.0 + scale) + shift


def _flip(a):
    return a[:, ::-1]


def split_in(p):
    sizes = [A_HEADS * A_DK, A_HEADS * A_DK, A_W, A_W, 2 * A_HEADS, 2 * A_HEADS,
             B_W, B_KV * B_HD, B_KV * B_HD, B_W, 2 * D_MODEL]
    idx = np.cumsum(sizes)[:-1].tolist()
    return jnp.split(p, idx, axis=-1)


def short_conv(x, w):
    ch = x.shape[-1]
    return lax.conv_general_dilated(
        x, w[:, None, :], window_strides=(1,), padding=((A_CONV // 2, A_CONV // 2),),
        dimension_numbers=('NWC', 'WIO', 'NWC'), feature_group_count=ch)


def gdn_prep(aq, ak, av, ab, aa, conv_w, a_log, dt_bias):
    b, t, _ = aq.shape
    qkv = jax.nn.silu(short_conv(jnp.concatenate([aq, ak, av], axis=-1), conv_w)).astype(jnp.float32)
    q, k, v = jnp.split(qkv, [A_HEADS * A_DK, 2 * A_HEADS * A_DK], axis=-1)
    q = l2_norm(q.reshape(b, t, A_HEADS, A_DK)) * (A_DK ** -0.5)
    k = l2_norm(k.reshape(b, t, A_HEADS, A_DK))
    v = v.reshape(b, t, A_HEADS, A_DV)
    beta = jax.nn.sigmoid(ab.astype(jnp.float32).reshape(b, t, 2, A_HEADS))
    g = -jnp.exp(a_log.astype(jnp.float32)) * jax.nn.softplus(
        aa.astype(jnp.float32).reshape(b, t, 2, A_HEADS) + dt_bias.astype(jnp.float32))
    return q, k, v, beta, g


def gated_delta_chunked(q, k, v, beta, g, s0):
    b, t, h, _ = q.shape
    dv = v.shape[-1]
    n = t // A_CHUNK

    def to_chunks(a):
        a = a.reshape((b, n, A_CHUNK, h) + a.shape[3:])
        return jnp.moveaxis(a, (1, 3), (0, 2))

    qc, kc, vc, bc, gc = (to_chunks(a) for a in (q, k, v, beta, g))
    gcum = jnp.cumsum(gc, axis=-1)
    idx = jnp.arange(A_CHUNK)
    incl = idx[:, None] >= idx[None, :]
    strict = idx[:, None] > idx[None, :]
    diff = gcum[..., :, None] - gcum[..., None, :]
    decay_incl = jnp.where(incl, jnp.exp(jnp.where(incl, diff, 0.0)), 0.0)
    decay_strict = jnp.where(strict, decay_incl, 0.0)
    kb = kc * bc[..., None]
    l_mat = jnp.einsum('nbhid,nbhjd->nbhij', kb, kc) * decay_strict
    eye = jnp.eye(A_CHUNK, dtype=jnp.float32)
    rhs = jnp.concatenate([vc * bc[..., None], kb * jnp.exp(gcum)[..., None]], axis=-1)
    sol = lax.linalg.triangular_solve(eye + l_mat, rhs, left_side=True, lower=True,
                                      unit_diagonal=True)
    u, w = sol[..., :dv], sol[..., dv:]
    attn_intra = jnp.einsum('nbhid,nbhjd->nbhij', qc, kc) * decay_incl
    q_dec = qc * jnp.exp(gcum)[..., None]
    k_dec = kc * jnp.exp(gcum[..., -1:] - gcum)[..., None]
    g_last = jnp.exp(gcum[..., -1])

    def step(s, xs):
        u_n, w_n, a_n, qd_n, kd_n, gl_n = xs
        v_new = u_n - jnp.einsum('bhcd,bhde->bhce', w_n, s)
        o_n = jnp.einsum('bhcd,bhde->bhce', qd_n, s) + jnp.einsum('bhij,bhje->bhie', a_n, v_new)
        s = s * gl_n[..., None, None] + jnp.einsum('bhcd,bhce->bhde', kd_n, v_new)
        return s, o_n

    s_fin, o = lax.scan(step, s0, (u, w, attn_intra, q_dec, k_dec, g_last))
    o = jnp.moveaxis(o, (0, 2), (1, 3)).reshape(b, t, h, dv)
    return o, s_fin


def gdn_output(o, z, norm_w):
    b, t, _ = z.shape
    y = rms_norm(o, norm_w) * jax.nn.silu(z.astype(jnp.float32)).reshape(b, t, A_HEADS, A_DV)
    return y.reshape(b, t, A_W).astype(z.dtype)


def head_rms(a, heads, w):
    b, t, _ = a.shape
    return rms_norm(a.reshape(b, t, heads, -1), w)


def axial_rope_tables(t):
    rows = t // GRID_W
    row = jnp.repeat(jnp.arange(rows), GRID_W).astype(jnp.float32)
    col = jnp.tile(jnp.arange(GRID_W), rows).astype(jnp.float32)
    n_freq = B_HD // 4
    inv = ROPE_THETA ** (-jnp.arange(n_freq, dtype=jnp.float32) / n_freq)
    ang = jnp.concatenate([row[:, None] * inv, col[:, None] * inv], axis=-1)
    return jnp.cos(ang), jnp.sin(ang)


def axial_rope(x, cos, sin):
    xf = x.astype(jnp.float32)
    x1, x2 = xf[..., :B_HD // 2], xf[..., B_HD // 2:]
    c, s = cos[None, :, None, :], sin[None, :, None, :]
    return jnp.concatenate([x1 * c - x2 * s, x1 * s + x2 * c], axis=-1).astype(x.dtype)


def block_attention(q, k, v):
    b, tq, h, hd = q.shape
    grp = h // B_KV
    nb = tq // Q_BLOCK
    qb = q.reshape(b, nb, Q_BLOCK, B_KV, grp, hd).transpose(1, 0, 2, 3, 4, 5)
    scale = hd ** -0.5

    def one_block(qi):
        s = jnp.einsum('bqkgd,bskd->bkgqs', qi, k).astype(jnp.float32) * scale
        p = jax.nn.softmax(s, axis=-1).astype(v.dtype)
        return jnp.einsum('bkgqs,bskd->bqkgd', p, v)

    o = lax.map(one_block, qb)
    return o.transpose(1, 0, 2, 3, 4, 5).reshape(b, tq, h * hd)


def merge_branches(a_out, b_out, gate_logits, w_proj_a, w_proj_b, w_out):
    g_a, g_b = jnp.split(jax.nn.sigmoid(gate_logits), 2, axis=-1)
    return (g_a * (a_out @ w_proj_a) + g_b * (b_out @ w_proj_b)) @ w_out


def hybrid_layer(x, ctx, c, c_ctx, w_mod, b_mod, norm_w, w_in, conv_w, a_log, dt_bias,
                 a_norm_w, q_norm_w, k_norm_w, w_proj_a, w_proj_b, w_out, update_ctx):
    b, t, _ = x.shape
    shift, scale, gate = jnp.split((jax.nn.silu(c) @ w_mod + b_mod)[:, None, :], 3, axis=-1)
    shift_c, scale_c, gate_c = jnp.split(jax.nn.silu(c_ctx) @ w_mod + b_mod, 3, axis=-1)
    h = modulate(x, norm_w, shift, scale)
    hc = modulate(ctx, norm_w, shift_c, scale_c)
    aq, ak, av, az, ab, aa, bq, bk, bv, bz, br = split_in(h @ w_in)
    caq, cak, cav, caz, cab, caa, cbq, cbk, cbv, cbz, cbr = split_in(hc @ w_in)

    ql, kl, vl, betal, gl = gdn_prep(aq, ak, av, ab, aa, conv_w, a_log, dt_bias)
    qc, kc, vc, betac, gc = gdn_prep(caq, cak, cav, cab, caa, conv_w, a_log, dt_bias)
    s0 = jnp.zeros((b, A_HEADS, A_DK, A_DV), jnp.float32)
    oc_f, s_f = gated_delta_chunked(qc, kc, vc, betac[:, :, 0], gc[:, :, 0], s0)
    oc_b, s_b = gated_delta_chunked(_flip(qc), _flip(kc), _flip(vc),
                                    _flip(betac[:, :, 1]), _flip(gc[:, :, 1]), s0)
    ol_f, _ = gated_delta_chunked(ql, kl, vl, betal[:, :, 0], gl[:, :, 0], s_f)
    ol_b, _ = gated_delta_chunked(_flip(ql), _flip(kl), _flip(vl),
                                  _flip(betal[:, :, 1]), _flip(gl[:, :, 1]), s_b)
    a_lat = gdn_output(ol_f + _flip(ol_b), az, a_norm_w)

    cos, sin = axial_rope_tables(t)
    q_lat = axial_rope(head_rms(bq, B_HEADS, q_norm_w), cos, sin)
    k_lat = axial_rope(head_rms(bk, B_KV, k_norm_w), cos, sin)
    v_lat = bv.reshape(b, t, B_KV, B_HD)
    k_ctx = head_rms(cbk, B_KV, k_norm_w)
    v_ctx = cbv.reshape(b, -1, B_KV, B_HD)
    keys = jnp.concatenate([k_ctx, k_lat], axis=1)
    vals = jnp.concatenate([v_ctx, v_lat], axis=1)
    b_lat = block_attention(q_lat, keys, vals) * jax.nn.silu(bz)

    x_new = x + gate * merge_branches(a_lat, b_lat, br, w_proj_a, w_proj_b, w_out)
    if update_ctx:
        a_ctx = gdn_output(oc_f + _flip(oc_b), caz, a_norm_w)
        q_ctx = head_rms(cbq, B_HEADS, q_norm_w)
        b_ctx = block_attention(q_ctx, k_ctx, v_ctx) * jax.nn.silu(cbz)
        ctx = ctx + gate_c * merge_branches(a_ctx, b_ctx, cbr, w_proj_a, w_proj_b, w_out)
    return x_new, ctx


def setup_inputs(seed: int = 0) -> dict:
    key = jax.random.key(seed)
    ks = jax.random.split(key, 18)
    f32 = jnp.float32
    nrm = lambda k, shape, s: jax.random.normal(k, shape, f32) * s
    dt = jnp.exp(jax.random.uniform(ks[9], (DEPTH, 2, A_HEADS), f32,
                                    minval=float(np.log(1e-3)), maxval=float(np.log(1e-1))))
    return {
        'x': nrm(ks[0], (BATCH, SEQ, D_MODEL), 1.0),
        'c': nrm(ks[1], (BATCH, D_MODEL), 1.0),
        'ctx': nrm(ks[2], (BATCH, CTX_LEN, D_MODEL), 1.0),
        'c_ctx': nrm(ks[3], (D_MODEL,), 1.0),
        'w_mod': nrm(ks[4], (DEPTH, D_MODEL, 3 * D_MODEL), 0.5 * D_MODEL ** -0.5),
        'b_mod': nrm(ks[5], (DEPTH, 3 * D_MODEL), 0.01),
        'norm_w': 1.0 + nrm(ks[6], (DEPTH, D_MODEL), 0.02),
        'w_in': nrm(ks[7], (DEPTH, D_MODEL, IN_COLS), D_MODEL ** -0.5),
        'conv_w': nrm(ks[8], (DEPTH, A_CONV, CONV_CH), A_CONV ** -0.5),
        'a_log': jnp.log(jax.random.uniform(ks[10], (DEPTH, 2, A_HEADS), f32, minval=1.0, maxval=16.0)),
        'dt_bias': dt + jnp.log(-jnp.expm1(-dt)),
        'a_norm_w': 1.0 + nrm(ks[11], (DEPTH, A_DV), 0.02),
        'q_norm_w': 1.0 + nrm(ks[12], (DEPTH, B_HD), 0.02),
        'k_norm_w': 1.0 + nrm(ks[13], (DEPTH, B_HD), 0.02),
        'w_proj_a': nrm(ks[14], (DEPTH, A_W, D_MODEL), A_W ** -0.5),
        'w_proj_b': nrm(ks[15], (DEPTH, B_W, D_MODEL), B_W ** -0.5),
        'w_out': nrm(ks[16], (DEPTH, D_MODEL, D_MODEL), D_MODEL ** -0.5),
    }


def reference(x, c, ctx, c_ctx, w_mod, b_mod, norm_w, w_in, conv_w, a_log, dt_bias,
              a_norm_w, q_norm_w, k_norm_w, w_proj_a, w_proj_b, w_out):
    for layer in range(DEPTH):
        x, ctx = hybrid_layer(x, ctx, c, c_ctx, w_mod[layer], b_mod[layer], norm_w[layer],
                              w_in[layer], conv_w[layer], a_log[layer], dt_bias[layer],
                              a_norm_w[layer], q_norm_w[layer], k_norm_w[layer],
                              w_proj_a[layer], w_proj_b[layer], w_out[layer],
                              update_ctx=layer + 1 < DEPTH)
    return x
```

```python
import functools

import jax
import jax.numpy as jnp
from jax import lax
from jax.experimental import pallas as pl
from jax.experimental.pallas import tpu as pltpu

F32 = jnp.float32
BF16 = jnp.bfloat16

D_MODEL = 2048
SEQ = 2048
CTX_LEN = 256
TOK = SEQ + CTX_LEN
GRID_W = 64
EPS = 1e-6
HEADS = 8
HD = 128
KV_HEADS = 2
A_CONV = 5
ROPE_THETA = 10000.0
CHUNK = 64

COL_AQ, COL_AK, COL_AV, COL_AZ, COL_BQ, COL_BZ, COL_BRA, COL_BRB, COL_SMALL = (
    0, 1024, 2048, 3072, 4096, 5120, 6144, 8192, 10240)
P_COLS = 11264
SMALL_BK, SMALL_BV, SMALL_BA = 0, 256, 512

VMEM_LIMIT = 56 * 1024 * 1024


def _silu(x):
    return x * jax.nn.sigmoid(x)


def _softplus(x):
    return jnp.maximum(x, 0.0) + jnp.log1p(jnp.exp(-jnp.abs(x)))


def _bdot(a, b):
    return jnp.dot(a.astype(BF16), b.astype(BF16), preferred_element_type=F32)


def _bdot_nt(a, b):
    return lax.dot_general(a.astype(BF16), b.astype(BF16), (((1,), (1,)), ((), ())),
                           preferred_element_type=F32)


def _bdot_tn(a, b):
    return lax.dot_general(a.astype(BF16), b.astype(BF16), (((0,), (0,)), ((), ())),
                           preferred_element_type=F32)


def _mod_kernel(c_ref, w_ref, b_ref, o_ref):
    o_ref[...] = _bdot(_silu(c_ref[...]), w_ref[...]) + b_ref[...]


def _mod_call(cc, w_mod, b_mod):
    tn = 768
    n = w_mod.shape[1]
    return pl.pallas_call(
        _mod_kernel,
        grid=(n // tn,),
        in_specs=[pl.BlockSpec((8, D_MODEL), lambda j: (0, 0)),
                  pl.BlockSpec((D_MODEL, tn), lambda j: (0, j)),
                  pl.BlockSpec((1, tn), lambda j: (0, j))],
        out_specs=pl.BlockSpec((8, tn), lambda j: (0, j)),
        out_shape=jax.ShapeDtypeStruct((8, n), F32),
        compiler_params=pltpu.CompilerParams(dimension_semantics=("arbitrary",),
                                             vmem_limit_bytes=VMEM_LIMIT),
        name="mod",
    )(cc, w_mod, b_mod)


def _inproj_kernel(x_ref, shift_ref, scale_ref, nw_ref, w_ref, *rest):
    o_ref, h_ref = rest[-2], rest[-1]

    @pl.when(pl.program_id(2) == 0)
    def _():
        x = x_ref[0]
        r = lax.rsqrt(jnp.mean(x * x, axis=-1, keepdims=True) + EPS)
        h = (x * r) * nw_ref[...] * (1.0 + scale_ref[0]) + shift_ref[0]
        h_ref[...] = h.astype(BF16)

    o_ref[0] = jnp.dot(h_ref[...], w_ref[...], preferred_element_type=F32).astype(BF16)


def _inproj_latent(x, mod3, norm_w, w_all):
    b, t, d = x.shape
    tm, tn = 1024, 1024
    return pl.pallas_call(
        _inproj_kernel,
        grid=(b, t // tm, P_COLS // tn),
        in_specs=[pl.BlockSpec((1, tm, d), lambda bi, i, j: (bi, i, 0)),
                  pl.BlockSpec((1, 1, d), lambda bi, i, j: (bi, 0, 0)),
                  pl.BlockSpec((1, 1, d), lambda bi, i, j: (bi, 0, 1)),
                  pl.BlockSpec((1, d), lambda bi, i, j: (0, 0)),
                  pl.BlockSpec((d, tn), lambda bi, i, j: (0, j))],
        out_specs=pl.BlockSpec((1, tm, tn), lambda bi, i, j: (bi, i, j)),
        out_shape=jax.ShapeDtypeStruct((b, TOK, P_COLS), BF16),
        scratch_shapes=[pltpu.VMEM((tm, d), BF16)],
        compiler_params=pltpu.CompilerParams(
            dimension_semantics=("arbitrary", "arbitrary", "arbitrary"),
            vmem_limit_bytes=VMEM_LIMIT),
        name="inproj_latent",
    )(x, mod3, mod3, norm_w, w_all)


def _inproj_ctx(ctx, mod3, norm_w, w_all, p):
    b, t, d = ctx.shape
    tn = 1024
    small_blk = COL_SMALL // tn

    def col(j):
        return jnp.where(j < 3, j, small_blk)

    return pl.pallas_call(
        _inproj_kernel,
        grid=(b, 1, 4),
        in_specs=[pl.BlockSpec((1, t, d), lambda bi, i, j: (bi, 0, 0)),
                  pl.BlockSpec((1, 1, d), lambda bi, i, j: (4, 0, 0)),
                  pl.BlockSpec((1, 1, d), lambda bi, i, j: (4, 0, 1)),
                  pl.BlockSpec((1, d), lambda bi, i, j: (0, 0)),
                  pl.BlockSpec((d, tn), lambda bi, i, j: (0, col(j))),
                  pl.BlockSpec(memory_space=pl.ANY)],
        out_specs=pl.BlockSpec((1, t, tn), lambda bi, i, j: (bi, SEQ // t, col(j))),
        out_shape=jax.ShapeDtypeStruct(p.shape, p.dtype),
        scratch_shapes=[pltpu.VMEM((t, d), BF16)],
        input_output_aliases={5: 0},
        compiler_params=pltpu.CompilerParams(
            dimension_semantics=("arbitrary", "arbitrary", "arbitrary"),
            vmem_limit_bytes=VMEM_LIMIT),
        name="inproj_ctx",
    )(ctx, mod3, mod3, norm_w, w_all, p)


PREP_T = 256
HALO = 16


def _prep_kernel(main_ref, prev_ref, next_ref, small_ref, convw_ref, alog_ref, dtb_ref, knw_ref,
                 cos_ref, sin_ref, g_ref, bg_ref, kn_ref, ext_ref):
    i = pl.program_id(1)
    ctx_tile = SEQ // PREP_T
    has_prev = jnp.logical_and(i != 0, i != ctx_tile)
    has_next = i < ctx_tile - 1
    pad = A_CONV // 2
    for c in range(3):
        cs = slice(c * 1024, (c + 1) * 1024)
        ext_ref[0:HALO, :] = jnp.where(has_prev, prev_ref[0, :, cs].astype(F32), 0.0)
        ext_ref[HALO:HALO + PREP_T, :] = main_ref[0, :, cs].astype(F32)
        ext_ref[HALO + PREP_T:, :] = jnp.where(has_next, next_ref[0, :, cs].astype(F32), 0.0)
        acc = jnp.zeros((PREP_T, 1024), F32)
        for k in range(A_CONV):
            acc = acc + ext_ref[HALO - pad + k:HALO - pad + k + PREP_T, :] * convw_ref[k:k + 1, cs]
        act = _silu(acc)
        if c == 2:
            g_ref[0, :, cs] = act.astype(BF16)
        else:
            mul = HD ** -0.5 if c == 0 else 1.0
            for h in range(HEADS):
                a = act[:, h * HD:(h + 1) * HD]
                r = lax.rsqrt(jnp.sum(a * a, axis=-1, keepdims=True) + EPS)
                g_ref[0, :, c * 1024 + h * HD:c * 1024 + (h + 1) * HD] = (a * (r * mul)).astype(BF16)

    ba = small_ref[0, :, SMALL_BA:SMALL_BA + 128].astype(F32)
    beta = jax.nn.sigmoid(ba)
    gdec = -jnp.exp(alog_ref[...]) * _softplus(ba + dtb_ref[...])
    lane = lax.broadcasted_iota(jnp.int32, ba.shape, 1)
    bg_ref[0] = jnp.where(lane < 2 * HEADS, beta, gdec)

    for h in range(KV_HEADS):
        kh = small_ref[0, :, SMALL_BK + h * HD:SMALL_BK + (h + 1) * HD].astype(F32)
        r = lax.rsqrt(jnp.mean(kh * kh, axis=-1, keepdims=True) + EPS)
        kn = kh * r * knw_ref[...]
        kr = kn * cos_ref[...] + pltpu.roll(kn, HD // 2, 1) * sin_ref[...]
        kn_ref[0, :, h * HD:(h + 1) * HD] = kr.astype(BF16)


def _prep_call(p, conv_w, alog_row, dtb_row, k_norm_w, cosf, sinf):
    b = p.shape[0]
    nt = TOK // PREP_T
    hb = PREP_T // HALO
    last_hb = TOK // HALO - 1
    return pl.pallas_call(
        _prep_kernel,
        grid=(b, nt),
        in_specs=[pl.BlockSpec((1, PREP_T, 3072), lambda bi, i: (bi, i, 0)),
                  pl.BlockSpec((1, HALO, 3072), lambda bi, i: (bi, jnp.maximum(i * hb - 1, 0), 0)),
                  pl.BlockSpec((1, HALO, 3072), lambda bi, i: (bi, jnp.minimum((i + 1) * hb, last_hb), 0)),
                  pl.BlockSpec((1, PREP_T, 1024), lambda bi, i: (bi, i, COL_SMALL // 1024)),
                  pl.BlockSpec((A_CONV, 3072), lambda bi, i: (0, 0)),
                  pl.BlockSpec((1, 128), lambda bi, i: (0, 0)),
                  pl.BlockSpec((1, 128), lambda bi, i: (0, 0)),
                  pl.BlockSpec((1, HD), lambda bi, i: (0, 0)),
                  pl.BlockSpec((PREP_T, HD), lambda bi, i: (i, 0)),
                  pl.BlockSpec((PREP_T, HD), lambda bi, i: (i, 0))],
        out_specs=[pl.BlockSpec((1, PREP_T, 3072), lambda bi, i: (bi, i, 0)),
                   pl.BlockSpec((1, PREP_T, 128), lambda bi, i: (bi, i, 0)),
                   pl.BlockSpec((1, PREP_T, KV_HEADS * HD), lambda bi, i: (bi, i, 0))],
        out_shape=[jax.ShapeDtypeStruct((b, TOK, 3072), BF16),
                   jax.ShapeDtypeStruct((b, TOK, 128), F32),
                   jax.ShapeDtypeStruct((b, TOK, KV_HEADS * HD), BF16)],
        scratch_shapes=[pltpu.VMEM((PREP_T + 2 * HALO, 1024), F32)],
        compiler_params=pltpu.CompilerParams(dimension_semantics=("arbitrary", "arbitrary"),
                                             vmem_limit_bytes=VMEM_LIMIT),
        name="prep",
    )(p, p, p, p, conv_w, alog_row, dtb_row, k_norm_w, cosf, sinf)


NC_CTX = CTX_LEN // CHUNK
NC_LAT = SEQ // CHUNK
NC = NC_CTX + NC_LAT


def _gdn_direction(qkv_ref, bg_ref, o_ref, s_ref, direction, store):
    c = CHUNK
    ii = lax.broadcasted_iota(jnp.int32, (c, c), 0)
    jj = lax.broadcasted_iota(jnp.int32, (c, c), 1)
    if direction == 0:
        incl, strict, last = ii >= jj, ii > jj, c - 1
    else:
        incl, strict, last = ii <= jj, ii < jj, 0
    eye = (ii == jj).astype(F32)

    bg = bg_ref[0]
    tri = incl.astype(BF16)
    g1 = bg.astype(BF16)
    r1 = bg - g1.astype(F32)
    g2 = r1.astype(BF16)
    g3 = (r1 - g2.astype(F32)).astype(BF16)
    gc = (jnp.dot(tri, g1, preferred_element_type=F32) + jnp.dot(tri, g2, preferred_element_type=F32)
          + jnp.dot(tri, g3, preferred_element_type=F32))
    gc_t = jnp.concatenate([gc, jnp.zeros((128 - c, 128), F32)], axis=0).T

    for h in range(HEADS):
        cb = direction * HEADS + h
        cg = 2 * HEADS + cb
        q = qkv_ref[0, :, h * HD:(h + 1) * HD]
        k = qkv_ref[0, :, 1024 + h * HD:1024 + (h + 1) * HD]
        v = qkv_ref[0, :, 2048 + h * HD:2048 + (h + 1) * HD]
        kf = k.astype(F32)
        beta = bg[:, cb:cb + 1]
        gcol = gc[:, cg:cg + 1]
        grow = gc_t[cg:cg + 1, 0:c]
        glast = gc[last:last + 1, cg:cg + 1]
        decay = jnp.where(incl, jnp.exp(jnp.where(incl, gcol - grow, 0.0)), 0.0)
        kk = _bdot_nt(k, k)
        qk = _bdot_nt(q, k)
        l_mat = beta * kk * jnp.where(strict, decay, 0.0)
        attn = qk * decay
        lp = l_mat
        t_inv = eye - l_mat
        n_sq = c.bit_length() - 2
        for _ in range(n_sq):
            lp = _bdot(lp, lp)
            t_inv = t_inv + _bdot(t_inv, lp)
        eg = jnp.exp(gcol)
        u = _bdot(t_inv, v.astype(F32) * beta)
        w = _bdot(t_inv, kf * (beta * eg))
        q_dec = q.astype(F32) * eg
        k_dec = kf * jnp.exp(glast - gcol)
        s = s_ref[h]
        v_new = u - _bdot(w, s)
        if store is not None:
            o = _bdot(q_dec, s) + _bdot(attn, v_new)

            @pl.when(store)
            def _():
                o_ref[0, 0, :, h * HD:(h + 1) * HD] = o.astype(BF16)

        s_ref[h] = s * jnp.exp(glast) + _bdot_tn(k_dec, v_new)


def _gdn_kernel(qkv_f_ref, bg_f_ref, qkv_b_ref, bg_b_ref, o_f_ref, o_b_ref, s_f_ref, s_b_ref):
    step = pl.program_id(1)

    @pl.when(step == 0)
    def _():
        s_f_ref[...] = jnp.zeros(s_f_ref.shape, F32)
        s_b_ref[...] = jnp.zeros(s_b_ref.shape, F32)

    store = step >= NC_CTX
    _gdn_direction(qkv_f_ref, bg_f_ref, o_f_ref, s_f_ref, 0, store)
    _gdn_direction(qkv_b_ref, bg_b_ref, o_b_ref, s_b_ref, 1, store)


def _gdn_call(g, bg):
    b = g.shape[0]
    lat0 = 0
    ctx0 = SEQ // CHUNK

    def fwd_blk(s):
        return jnp.where(s < NC_CTX, ctx0 + s, lat0 + s - NC_CTX)

    def bwd_blk(s):
        return jnp.where(s < NC_CTX, ctx0 + NC_CTX - 1 - s, lat0 + NC_LAT - 1 - (s - NC_CTX))

    def fwd_out(s):
        return jnp.maximum(s - NC_CTX, 0)

    def bwd_out(s):
        return NC_LAT - 1 - jnp.maximum(s - NC_CTX, 0)

    return pl.pallas_call(
        _gdn_kernel,
        grid=(b, NC),
        in_specs=[pl.BlockSpec((1, CHUNK, 3072), lambda bi, s: (bi, fwd_blk(s), 0)),
                  pl.BlockSpec((1, CHUNK, 128), lambda bi, s: (bi, fwd_blk(s), 0)),
                  pl.BlockSpec((1, CHUNK, 3072), lambda bi, s: (bi, bwd_blk(s), 0)),
                  pl.BlockSpec((1, CHUNK, 128), lambda bi, s: (bi, bwd_blk(s), 0))],
        out_specs=[pl.BlockSpec((1, 1, CHUNK, 1024), lambda bi, s: (0, bi, fwd_out(s), 0)),
                   pl.BlockSpec((1, 1, CHUNK, 1024), lambda bi, s: (0, bi, bwd_out(s), 0))],
        out_shape=[jax.ShapeDtypeStruct((1, b, SEQ, 1024), BF16),
                   jax.ShapeDtypeStruct((1, b, SEQ, 1024), BF16)],
        scratch_shapes=[pltpu.VMEM((HEADS, HD, HD), F32), pltpu.VMEM((HEADS, HD, HD), F32)],
        compiler_params=pltpu.CompilerParams(dimension_semantics=("arbitrary", "arbitrary"),
                                             vmem_limit_bytes=VMEM_LIMIT),
        name="gdn",
    )(g, bg, g, bg)


ATT_TQ = 256
Q_PER_KV = HEADS // KV_HEADS


def _attn_kernel(q_ref, z_ref, k_ref, v_ref, qnw_ref, cos_ref, sin_ref, o_ref):
    k = k_ref[0]
    v = v_ref[0]
    for j in range(Q_PER_KV):
        sl = slice(j * HD, (j + 1) * HD)
        qh = q_ref[0, :, sl].astype(F32)
        r = lax.rsqrt(jnp.mean(qh * qh, axis=-1, keepdims=True) + EPS)
        qn = qh * r * qnw_ref[...]
        qr = qn * cos_ref[...] + pltpu.roll(qn, HD // 2, 1) * sin_ref[...]
        s = _bdot_nt(qr * (HD ** -0.5), k)
        m = jnp.max(s, axis=-1, keepdims=True)
        e = jnp.exp(s - m)
        den = jnp.sum(e, axis=-1, keepdims=True)
        o = jnp.dot(e.astype(BF16), v, preferred_element_type=F32) / den
        o_ref[0, :, sl] = (o * _silu(z_ref[0, :, sl].astype(F32))).astype(BF16)


def _attn_call(p, kn, q_norm_w, cosf, sinf):
    b = p.shape[0]
    qw = Q_PER_KV * HD
    return pl.pallas_call(
        _attn_kernel,
        grid=(b, KV_HEADS, SEQ // ATT_TQ),
        in_specs=[pl.BlockSpec((1, ATT_TQ, qw), lambda bi, g, i: (bi, i, COL_BQ // qw + g)),
                  pl.BlockSpec((1, ATT_TQ, qw), lambda bi, g, i: (bi, i, COL_BZ // qw + g)),
                  pl.BlockSpec((1, TOK, HD), lambda bi, g, i: (bi, 0, g)),
                  pl.BlockSpec((1, TOK, HD), lambda bi, g, i: (bi, 0, (COL_SMALL + SMALL_BV) // HD + g)),
                  pl.BlockSpec((1, HD), lambda bi, g, i: (0, 0)),
                  pl.BlockSpec((ATT_TQ, HD), lambda bi, g, i: (i, 0)),
                  pl.BlockSpec((ATT_TQ, HD), lambda bi, g, i: (i, 0))],
        out_specs=pl.BlockSpec((1, ATT_TQ, qw), lambda bi, g, i: (bi, i, g)),
        out_shape=jax.ShapeDtypeStruct((b, SEQ, HEADS * HD), BF16),
        compiler_params=pltpu.CompilerParams(
            dimension_semantics=("arbitrary", "arbitrary", "arbitrary"),
            vmem_limit_bytes=VMEM_LIMIT),
        name="attn",
    )(p, p, kn, p, q_norm_w, cosf, sinf)


MERGE_T = 256


def _merge_kernel(of_ref, ob_ref, az_ref, bl_ref, bra_ref, brb_ref, x_ref, gate_ref, anw_ref,
                  wpa_ref, wpb_ref, wo_ref, out_ref, a_ref):
    o = of_ref[0, 0].astype(F32) + ob_ref[0, 0].astype(F32)
    for h in range(HEADS):
        sl = slice(h * HD, (h + 1) * HD)
        oh = o[:, sl]
        r = lax.rsqrt(jnp.mean(oh * oh, axis=-1, keepdims=True) + EPS)
        a_ref[:, sl] = (oh * r * anw_ref[...] * _silu(az_ref[0, :, sl].astype(F32))).astype(BF16)
    pa = jnp.dot(a_ref[...], wpa_ref[...], preferred_element_type=F32)
    pb = jnp.dot(bl_ref[0], wpb_ref[...], preferred_element_type=F32)
    m = (jax.nn.sigmoid(bra_ref[0].astype(F32)) * pa + jax.nn.sigmoid(brb_ref[0].astype(F32)) * pb)
    y = jnp.dot(m.astype(BF16), wo_ref[...], preferred_element_type=F32)
    out_ref[0] = x_ref[0] + gate_ref[0] * y


def _merge_call(o_f, o_b, p, b_lat, x, mod3, a_norm_w, wpa, wpb, wo):
    b, t, d = x.shape
    tm = MERGE_T
    return pl.pallas_call(
        _merge_kernel,
        grid=(b, t // tm),
        in_specs=[pl.BlockSpec((1, 1, tm, 1024), lambda bi, i: (0, bi, i, 0)),
                  pl.BlockSpec((1, 1, tm, 1024), lambda bi, i: (0, bi, i, 0)),
                  pl.BlockSpec((1, tm, 1024), lambda bi, i: (bi, i, COL_AZ // 1024)),
                  pl.BlockSpec((1, tm, 1024), lambda bi, i: (bi, i, 0)),
                  pl.BlockSpec((1, tm, d), lambda bi, i: (bi, i, COL_BRA // d)),
                  pl.BlockSpec((1, tm, d), lambda bi, i: (bi, i, COL_BRB // d)),
                  pl.BlockSpec((1, tm, d), lambda bi, i: (bi, i, 0)),
                  pl.BlockSpec((1, 1, d), lambda bi, i: (bi, 0, 2)),
                  pl.BlockSpec((1, HD), lambda bi, i: (0, 0)),
                  pl.BlockSpec((1024, d), lambda bi, i: (0, 0)),
                  pl.BlockSpec((1024, d), lambda bi, i: (0, 0)),
                  pl.BlockSpec((d, d), lambda bi, i: (0, 0))],
        out_specs=pl.BlockSpec((1, tm, d), lambda bi, i: (bi, i, 0)),
        out_shape=jax.ShapeDtypeStruct((b, t, d), F32),
        scratch_shapes=[pltpu.VMEM((tm, 1024), BF16)],
        compiler_params=pltpu.CompilerParams(dimension_semantics=("arbitrary", "arbitrary"),
                                             vmem_limit_bytes=VMEM_LIMIT),
        name="merge",
    )(o_f, o_b, p, b_lat, p, p, x, mod3, a_norm_w, wpa, wpb, wo)


def _rope_tables():
    t = jnp.arange(SEQ)
    row = (t // GRID_W).astype(F32)
    col = (t % GRID_W).astype(F32)
    n_freq = HD // 4
    inv = ROPE_THETA ** (-jnp.arange(n_freq, dtype=F32) / n_freq)
    ang = jnp.concatenate([row[:, None] * inv, col[:, None] * inv], axis=-1)
    cos, sin = jnp.cos(ang), jnp.sin(ang)
    cosf = jnp.concatenate([cos, cos], axis=-1)
    sinf = jnp.concatenate([-sin, sin], axis=-1)
    cosf = jnp.concatenate([cosf, jnp.ones((CTX_LEN, HD), F32)], axis=0)
    sinf = jnp.concatenate([sinf, jnp.zeros((CTX_LEN, HD), F32)], axis=0)
    return cosf, sinf


def _gate_row(a):
    return jnp.zeros((1, 128), F32).at[0, 2 * HEADS:4 * HEADS].set(a.reshape(-1).astype(F32))


def _layer(x, ctx, c, c_ctx, w_mod, b_mod, norm_w, w_in, conv_w, a_log, dt_bias,
           a_norm_w, q_norm_w, k_norm_w, w_proj_a, w_proj_b, w_out):
    b = x.shape[0]
    d = D_MODEL
    w_all = jnp.concatenate(
        [w_in[:, 0:4096], w_in[:, 4128:5152], w_in[:, 5664:6688], w_in[:, 6688:10784],
         w_in[:, 5152:5664], w_in[:, 4096:4128],
         jnp.zeros((d, P_COLS - COL_SMALL - 544), w_in.dtype)], axis=1).astype(BF16)
    wpa = w_proj_a.astype(BF16)
    wpb = w_proj_b.astype(BF16)
    wo = w_out.astype(BF16)

    cc = jnp.concatenate([c, c_ctx[None, :], jnp.zeros((8 - b - 1, d), F32)], axis=0)
    mod = _mod_call(cc, w_mod, b_mod[None, :])
    mod3 = mod.reshape(8, 1, 3 * d)

    nw = norm_w[None, :]
    p = _inproj_latent(x, mod3, nw, w_all)
    p = _inproj_ctx(ctx, mod3, nw, w_all, p)

    cosf, sinf = _rope_tables()
    g, bg, kn = _prep_call(p, conv_w, _gate_row(a_log), _gate_row(dt_bias), k_norm_w[None, :], cosf, sinf)
    o_f, o_b = _gdn_call(g, bg)
    b_lat = _attn_call(p, kn, q_norm_w[None, :], cosf, sinf)
    return _merge_call(o_f, o_b, p, b_lat, x, mod3, a_norm_w[None, :], wpa, wpb, wo)


def kernel(x, c, ctx, c_ctx, w_mod, b_mod, norm_w, w_in, conv_w, a_log, dt_bias, a_norm_w,
           q_norm_w, k_norm_w, w_proj_a, w_proj_b, w_out):
    assert w_mod.shape[0] == 1
    return _layer(x, ctx, c, c_ctx, w_mod[0], b_mod[0], norm_w[0], w_in[0], conv_w[0], a_log[0],
                  dt_bias[0], a_norm_w[0], q_norm_w[0], k_norm_w[0], w_proj_a[0], w_proj_b[0],
                  w_out[0])
```

```python
import functools

import jax
import jax.numpy as jnp
from jax import lax
from jax.experimental import pallas as pl
from jax.experimental.pallas import tpu as pltpu

F32 = jnp.float32
BF16 = jnp.bfloat16

D_MODEL = 2048
SEQ = 2048
CTX_LEN = 256
TOK = SEQ + CTX_LEN
GRID_W = 64
EPS = 1e-6
HEADS = 8
HD = 128
KV_HEADS = 2
A_CONV = 5
ROPE_THETA = 10000.0
CHUNK = 64

COL_AQ, COL_AK, COL_AV, COL_AZ, COL_BQ, COL_BZ, COL_BRA, COL_BRB, COL_SMALL = (
    0, 1024, 2048, 3072, 4096, 5120, 6144, 8192, 10240)
P_COLS = 11264
SMALL_BK, SMALL_BV, SMALL_BA = 0, 256, 512

VMEM_LIMIT = 56 * 1024 * 1024


def _silu(x):
    return x * jax.nn.sigmoid(x)


def _softplus(x):
    return jnp.maximum(x, 0.0) + jnp.log1p(jnp.exp(-jnp.abs(x)))


def _bdot(a, b):
    return jnp.dot(a.astype(BF16), b.astype(BF16), preferred_element_type=F32)


def _bdot_nt(a, b):
    return lax.dot_general(a.astype(BF16), b.astype(BF16), (((1,), (1,)), ((), ())),
                           preferred_element_type=F32)


def _bdot_tn(a, b):
    return lax.dot_general(a.astype(BF16), b.astype(BF16), (((0,), (0,)), ((), ())),
                           preferred_element_type=F32)


def _mod_kernel(c_ref, w_ref, b_ref, o_ref):
    o_ref[...] = _bdot(_silu(c_ref[...]), w_ref[...]) + b_ref[...]


def _mod_call(cc, w_mod, b_mod):
    tn = 768
    n = w_mod.shape[1]
    return pl.pallas_call(
        _mod_kernel,
        grid=(n // tn,),
        in_specs=[pl.BlockSpec((8, D_MODEL), lambda j: (0, 0)),
                  pl.BlockSpec((D_MODEL, tn), lambda j: (0, j)),
                  pl.BlockSpec((1, tn), lambda j: (0, j))],
        out_specs=pl.BlockSpec((8, tn), lambda j: (0, j)),
        out_shape=jax.ShapeDtypeStruct((8, n), F32),
        compiler_params=pltpu.CompilerParams(dimension_semantics=("arbitrary",),
                                             vmem_limit_bytes=VMEM_LIMIT),
        name="mod",
    )(cc, w_mod, b_mod)


def _inproj_kernel(x_ref, shift_ref, scale_ref, nw_ref, w_ref, *rest):
    o_ref, h_ref = rest[-2], rest[-1]

    @pl.when(pl.program_id(2) == 0)
    def _():
        x = x_ref[0]
        r = lax.rsqrt(jnp.mean(x * x, axis=-1, keepdims=True) + EPS)
        h = (x * r) * nw_ref[...] * (1.0 + scale_ref[0]) + shift_ref[0]
        h_ref[...] = h.astype(BF16)

    o_ref[0] = jnp.dot(h_ref[...], w_ref[...], preferred_element_type=F32).astype(BF16)


def _inproj_latent(x, mod3, norm_w, w_all):
    b, t, d = x.shape
    tm, tn = 1024, 1024
    return pl.pallas_call(
        _inproj_kernel,
        grid=(b, t // tm, P_COLS // tn),
        in_specs=[pl.BlockSpec((1, tm, d), lambda bi, i, j: (bi, i, 0)),
                  pl.BlockSpec((1, 1, d), lambda bi, i, j: (bi, 0, 0)),
                  pl.BlockSpec((1, 1, d), lambda bi, i, j: (bi, 0, 1)),
                  pl.BlockSpec((1, d), lambda bi, i, j: (0, 0)),
                  pl.BlockSpec((d, tn), lambda bi, i, j: (0, j))],
        out_specs=pl.BlockSpec((1, tm, tn), lambda bi, i, j: (bi, i, j)),
        out_shape=jax.ShapeDtypeStruct((b, TOK, P_COLS), BF16),
        scratch_shapes=[pltpu.VMEM((tm, d), BF16)],
        compiler_params=pltpu.CompilerParams(
            dimension_semantics=("arbitrary", "arbitrary", "arbitrary"),
            vmem_limit_bytes=VMEM_LIMIT),
        name="inproj_latent",
    )(x, mod3, mod3, norm_w, w_all)


def _inproj_ctx(ctx, mod3, norm_w, w_all, p):
    b, t, d = ctx.shape
    tn = 1024
    small_blk = COL_SMALL // tn

    def col(j):
        return jnp.where(j < 3, j, small_blk)

    return pl.pallas_call(
        _inproj_kernel,
        grid=(b, 1, 4),
        in_specs=[pl.BlockSpec((1, t, d), lambda bi, i, j: (bi, 0, 0)),
                  pl.BlockSpec((1, 1, d), lambda bi, i, j: (4, 0, 0)),
                  pl.BlockSpec((1, 1, d), lambda bi, i, j: (4, 0, 1)),
                  pl.BlockSpec((1, d), lambda bi, i, j: (0, 0)),
                  pl.BlockSpec((d, tn), lambda bi, i, j: (0, col(j))),
                  pl.BlockSpec(memory_space=pl.ANY)],
        out_specs=pl.BlockSpec((1, t, tn), lambda bi, i, j: (bi, SEQ // t, col(j))),
        out_shape=jax.ShapeDtypeStruct(p.shape, p.dtype),
        scratch_shapes=[pltpu.VMEM((t, d), BF16)],
        input_output_aliases={5: 0},
        compiler_params=pltpu.CompilerParams(
            dimension_semantics=("arbitrary", "arbitrary", "arbitrary"),
            vmem_limit_bytes=VMEM_LIMIT),
        name="inproj_ctx",
    )(ctx, mod3, mod3, norm_w, w_all, p)


PREP_T = 256
HALO = 16


def _prep_kernel(main_ref, prev_ref, next_ref, small_ref, convw_ref, alog_ref, dtb_ref, knw_ref,
                 cos_ref, sin_ref, g_ref, bg_ref, kn_ref, ext_ref):
    i = pl.program_id(1)
    ctx_tile = SEQ // PREP_T
    has_prev = jnp.logical_and(i != 0, i != ctx_tile)
    has_next = i < ctx_tile - 1
    pad = A_CONV // 2
    for c in range(3):
        cs = slice(c * 1024, (c + 1) * 1024)
        ext_ref[0:HALO, :] = jnp.where(has_prev, prev_ref[0, :, cs].astype(F32), 0.0)
        ext_ref[HALO:HALO + PREP_T, :] = main_ref[0, :, cs].astype(F32)
        ext_ref[HALO + PREP_T:, :] = jnp.where(has_next, next_ref[0, :, cs].astype(F32), 0.0)
        acc = jnp.zeros((PREP_T, 1024), F32)
        for k in range(A_CONV):
            acc = acc + ext_ref[HALO - pad + k:HALO - pad + k + PREP_T, :] * convw_ref[k:k + 1, cs]
        act = _silu(acc)
        if c == 2:
            g_ref[0, :, cs] = act.astype(BF16)
        else:
            mul = HD ** -0.5 if c == 0 else 1.0
            for h in range(HEADS):
                a = act[:, h * HD:(h + 1) * HD]
                r = lax.rsqrt(jnp.sum(a * a, axis=-1, keepdims=True) + EPS)
                g_ref[0, :, c * 1024 + h * HD:c * 1024 + (h + 1) * HD] = (a * (r * mul)).astype(BF16)

    ba = small_ref[0, :, SMALL_BA:SMALL_BA + 128].astype(F32)
    beta = jax.nn.sigmoid(ba)
    gdec = -jnp.exp(alog_ref[...]) * _softplus(ba + dtb_ref[...])
    lane = lax.broadcasted_iota(jnp.int32, ba.shape, 1)
    bg_ref[0] = jnp.where(lane < 2 * HEADS, beta, gdec)

    for h in range(KV_HEADS):
        kh = small_ref[0, :, SMALL_BK + h * HD:SMALL_BK + (h + 1) * HD].astype(F32)
        r = lax.rsqrt(jnp.mean(kh * kh, axis=-1, keepdims=True) + EPS)
        kn = kh * r * knw_ref[...]
        kr = kn * cos_ref[...] + pltpu.roll(kn, HD // 2, 1) * sin_ref[...]
        kn_ref[0, :, h * HD:(h + 1) * HD] = kr.astype(BF16)


def _prep_call(p, conv_w, alog_row, dtb_row, k_norm_w, cosf, sinf):
    b = p.shape[0]
    nt = TOK // PREP_T
    hb = PREP_T // HALO
    last_hb = TOK // HALO - 1
    return pl.pallas_call(
        _prep_kernel,
        grid=(b, nt),
        in_specs=[pl.BlockSpec((1, PREP_T, 3072), lambda bi, i: (bi, i, 0)),
                  pl.BlockSpec((1, HALO, 3072), lambda bi, i: (bi, jnp.maximum(i * hb - 1, 0), 0)),
                  pl.BlockSpec((1, HALO, 3072), lambda bi, i: (bi, jnp.minimum((i + 1) * hb, last_hb), 0)),
                  pl.BlockSpec((1, PREP_T, 1024), lambda bi, i: (bi, i, COL_SMALL // 1024)),
                  pl.BlockSpec((A_CONV, 3072), lambda bi, i: (0, 0)),
                  pl.BlockSpec((1, 128), lambda bi, i: (0, 0)),
                  pl.BlockSpec((1, 128), lambda bi, i: (0, 0)),
                  pl.BlockSpec((1, HD), lambda bi, i: (0, 0)),
                  pl.BlockSpec((PREP_T, HD), lambda bi, i: (i, 0)),
                  pl.BlockSpec((PREP_T, HD), lambda bi, i: (i, 0))],
        out_specs=[pl.BlockSpec((1, PREP_T, 3072), lambda bi, i: (bi, i, 0)),
                   pl.BlockSpec((1, PREP_T, 128), lambda bi, i: (bi, i, 0)),
                   pl.BlockSpec((1, PREP_T, KV_HEADS * HD), lambda bi, i: (bi, i, 0))],
        out_shape=[jax.ShapeDtypeStruct((b, TOK, 3072), BF16),
                   jax.ShapeDtypeStruct((b, TOK, 128), F32),
                   jax.ShapeDtypeStruct((b, TOK, KV_HEADS * HD), BF16)],
        scratch_shapes=[pltpu.VMEM((PREP_T + 2 * HALO, 1024), F32)],
        compiler_params=pltpu.CompilerParams(dimension_semantics=("arbitrary", "arbitrary"),
                                             vmem_limit_bytes=VMEM_LIMIT),
        name="prep",
    )(p, p, p, p, conv_w, alog_row, dtb_row, k_norm_w, cosf, sinf)


NC_CTX = CTX_LEN // CHUNK
NC_LAT = SEQ // CHUNK
NC = NC_CTX + NC_LAT


def _chunk_masks(direction):
    c = CHUNK
    ii = lax.broadcasted_iota(jnp.int32, (c, c), 0)
    jj = lax.broadcasted_iota(jnp.int32, (c, c), 1)
    if direction == 0:
        return ii >= jj, ii > jj, c - 1
    return ii <= jj, ii < jj, 0


def _chunk_cumsum(bg, incl):
    tri = incl.astype(BF16)
    g1 = bg.astype(BF16)
    r1 = bg - g1.astype(F32)
    g2 = r1.astype(BF16)
    g3 = (r1 - g2.astype(F32)).astype(BF16)
    return (jnp.dot(tri, g1, preferred_element_type=F32) + jnp.dot(tri, g2, preferred_element_type=F32)
            + jnp.dot(tri, g3, preferred_element_type=F32))


def _gdn_kernel(qkv_f_ref, bg_f_ref, qkv_b_ref, bg_b_ref, o_f_ref, o_b_ref, s_f_ref, s_b_ref):
    step = pl.program_id(1)

    @pl.when(step == 0)
    def _():
        s_f_ref[...] = jnp.zeros(s_f_ref.shape, F32)
        s_b_ref[...] = jnp.zeros(s_b_ref.shape, F32)

    c = CHUNK
    qkv_refs, s_refs, o_refs = (qkv_f_ref, qkv_b_ref), (s_f_ref, s_b_ref), (o_f_ref, o_b_ref)
    chains = [(d, h) for d in range(2) for h in range(HEADS)]
    ii = lax.broadcasted_iota(jnp.int32, (c, c), 0)
    jj = lax.broadcasted_iota(jnp.int32, (c, c), 1)
    eye = (ii == jj).astype(F32)

    masks = [_chunk_masks(d) for d in range(2)]
    bgs = [bg_f_ref[0], bg_b_ref[0]]
    gcs = [_chunk_cumsum(bgs[d], masks[d][0]) for d in range(2)]
    gcts = [jnp.concatenate([gcs[d], jnp.zeros((128 - c, 128), F32)], axis=0).T for d in range(2)]

    q, k, v, kf, beta, gcol, glast, decay = {}, {}, {}, {}, {}, {}, {}, {}
    for d, h in chains:
        incl, _, last = masks[d]
        cb = d * HEADS + h
        cg = 2 * HEADS + cb
        q[d, h] = qkv_refs[d][0, :, h * HD:(h + 1) * HD]
        k[d, h] = qkv_refs[d][0, :, 1024 + h * HD:1024 + (h + 1) * HD]
        v[d, h] = qkv_refs[d][0, :, 2048 + h * HD:2048 + (h + 1) * HD]
        kf[d, h] = k[d, h].astype(F32)
        beta[d, h] = bgs[d][:, cb:cb + 1]
        gcol[d, h] = gcs[d][:, cg:cg + 1]
        grow = gcts[d][cg:cg + 1, 0:c]
        glast[d, h] = gcs[d][last:last + 1, cg:cg + 1]
        decay[d, h] = jnp.where(incl, jnp.exp(jnp.where(incl, gcol[d, h] - grow, 0.0)), 0.0)

    kk = {ch: _bdot_nt(k[ch], k[ch]) for ch in chains}
    qk = {ch: _bdot_nt(q[ch], k[ch]) for ch in chains}
    l_mat = {ch: beta[ch] * kk[ch] * jnp.where(masks[ch[0]][1], decay[ch], 0.0) for ch in chains}
    attn = {ch: qk[ch] * decay[ch] for ch in chains}

    lp = dict(l_mat)
    t_inv = {ch: eye - l_mat[ch] for ch in chains}
    for _ in range(c.bit_length() - 2):
        lp = {ch: _bdot(lp[ch], lp[ch]) for ch in chains}
        t_inv = {ch: t_inv[ch] + _bdot(t_inv[ch], lp[ch]) for ch in chains}

    eg = {ch: jnp.exp(gcol[ch]) for ch in chains}
    rhs = {ch: jnp.concatenate([v[ch].astype(F32) * beta[ch], kf[ch] * (beta[ch] * eg[ch])], axis=1)
           for ch in chains}
    uw = {ch: _bdot(t_inv[ch], rhs[ch]) for ch in chains}
    s = {ch: s_refs[ch[0]][ch[1]] for ch in chains}
    lhs = {ch: jnp.concatenate([uw[ch][:, HD:], q[ch].astype(F32) * eg[ch]], axis=0) for ch in chains}
    ws_qs = {ch: _bdot(lhs[ch], s[ch]) for ch in chains}
    v_new = {ch: uw[ch][:, :HD] - ws_qs[ch][:c] for ch in chains}
    av = {ch: _bdot(attn[ch], v_new[ch]) for ch in chains}
    k_dec = {ch: kf[ch] * jnp.exp(glast[ch] - gcol[ch]) for ch in chains}
    kv = {ch: _bdot_tn(k_dec[ch], v_new[ch]) for ch in chains}
    for d, h in chains:
        o_refs[d][0, 0, :, h * HD:(h + 1) * HD] = (ws_qs[d, h][c:] + av[d, h]).astype(BF16)
        s_refs[d][h] = s[d, h] * jnp.exp(glast[d, h]) + kv[d, h]


def _gdn_call(g, bg):
    b = g.shape[0]
    lat0 = 0
    ctx0 = SEQ // CHUNK

    def fwd_blk(s):
        return jnp.where(s < NC_CTX, ctx0 + s, lat0 + s - NC_CTX)

    def bwd_blk(s):
        return jnp.where(s < NC_CTX, ctx0 + NC_CTX - 1 - s, lat0 + NC_LAT - 1 - (s - NC_CTX))

    def fwd_out(s):
        return jnp.maximum(s - NC_CTX, 0)

    def bwd_out(s):
        return NC_LAT - 1 - jnp.maximum(s - NC_CTX, 0)

    return pl.pallas_call(
        _gdn_kernel,
        grid=(b, NC),
        in_specs=[pl.BlockSpec((1, CHUNK, 3072), lambda bi, s: (bi, fwd_blk(s), 0)),
                  pl.BlockSpec((1, CHUNK, 128), lambda bi, s: (bi, fwd_blk(s), 0)),
                  pl.BlockSpec((1, CHUNK, 3072), lambda bi, s: (bi, bwd_blk(s), 0)),
                  pl.BlockSpec((1, CHUNK, 128), lambda bi, s: (bi, bwd_blk(s), 0))],
        out_specs=[pl.BlockSpec((1, 1, CHUNK, 1024), lambda bi, s: (0, bi, fwd_out(s), 0)),
                   pl.BlockSpec((1, 1, CHUNK, 1024), lambda bi, s: (0, bi, bwd_out(s), 0))],
        out_shape=[jax.ShapeDtypeStruct((1, b, SEQ, 1024), BF16),
                   jax.ShapeDtypeStruct((1, b, SEQ, 1024), BF16)],
        scratch_shapes=[pltpu.VMEM((HEADS, HD, HD), F32), pltpu.VMEM((HEADS, HD, HD), F32)],
        compiler_params=pltpu.CompilerParams(dimension_semantics=("arbitrary", "arbitrary"),
                                             vmem_limit_bytes=VMEM_LIMIT),
        name="gdn",
    )(g, bg, g, bg)


ATT_TQ = 256
Q_PER_KV = HEADS // KV_HEADS


def _attn_kernel(q_ref, z_ref, k_ref, v_ref, qnw_ref, cos_ref, sin_ref, o_ref):
    k = k_ref[0]
    v = v_ref[0]
    for j in range(Q_PER_KV):
        sl = slice(j * HD, (j + 1) * HD)
        qh = q_ref[0, :, sl].astype(F32)
        r = lax.rsqrt(jnp.mean(qh * qh, axis=-1, keepdims=True) + EPS)
        qn = qh * r * qnw_ref[...]
        qr = qn * cos_ref[...] + pltpu.roll(qn, HD // 2, 1) * sin_ref[...]
        s = _bdot_nt(qr * (HD ** -0.5), k)
        m = jnp.max(s, axis=-1, keepdims=True)
        e = jnp.exp(s - m)
        den = jnp.sum(e, axis=-1, keepdims=True)
        o = jnp.dot(e.astype(BF16), v, preferred_element_type=F32) / den
        o_ref[0, :, sl] = (o * _silu(z_ref[0, :, sl].astype(F32))).astype(BF16)


def _attn_call(p, kn, q_norm_w, cosf, sinf):
    b = p.shape[0]
    qw = Q_PER_KV * HD
    return pl.pallas_call(
        _attn_kernel,
        grid=(b, KV_HEADS, SEQ // ATT_TQ),
        in_specs=[pl.BlockSpec((1, ATT_TQ, qw), lambda bi, g, i: (bi, i, COL_BQ // qw + g)),
                  pl.BlockSpec((1, ATT_TQ, qw), lambda bi, g, i: (bi, i, COL_BZ // qw + g)),
                  pl.BlockSpec((1, TOK, HD), lambda bi, g, i: (bi, 0, g)),
                  pl.BlockSpec((1, TOK, HD), lambda bi, g, i: (bi, 0, (COL_SMALL + SMALL_BV) // HD + g)),
                  pl.BlockSpec((1, HD), lambda bi, g, i: (0, 0)),
                  pl.BlockSpec((ATT_TQ, HD), lambda bi, g, i: (i, 0)),
                  pl.BlockSpec((ATT_TQ, HD), lambda bi, g, i: (i, 0))],
        out_specs=pl.BlockSpec((1, ATT_TQ, qw), lambda bi, g, i: (bi, i, g)),
        out_shape=jax.ShapeDtypeStruct((b, SEQ, HEADS * HD), BF16),
        compiler_params=pltpu.CompilerParams(
            dimension_semantics=("arbitrary", "arbitrary", "arbitrary"),
            vmem_limit_bytes=VMEM_LIMIT),
        name="attn",
    )(p, p, kn, p, q_norm_w, cosf, sinf)


MERGE_T = 256


def _merge_kernel(of_ref, ob_ref, az_ref, bl_ref, bra_ref, brb_ref, x_ref, gate_ref, anw_ref,
                  wpa_ref, wpb_ref, wo_ref, out_ref, a_ref):
    o = of_ref[0, 0].astype(F32) + ob_ref[0, 0].astype(F32)
    for h in range(HEADS):
        sl = slice(h * HD, (h + 1) * HD)
        oh = o[:, sl]
        r = lax.rsqrt(jnp.mean(oh * oh, axis=-1, keepdims=True) + EPS)
        a_ref[:, sl] = (oh * r * anw_ref[...] * _silu(az_ref[0, :, sl].astype(F32))).astype(BF16)
    pa = jnp.dot(a_ref[...], wpa_ref[...], preferred_element_type=F32)
    pb = jnp.dot(bl_ref[0], wpb_ref[...], preferred_element_type=F32)
    m = (jax.nn.sigmoid(bra_ref[0].astype(F32)) * pa + jax.nn.sigmoid(brb_ref[0].astype(F32)) * pb)
    y = jnp.dot(m.astype(BF16), wo_ref[...], preferred_element_type=F32)
    out_ref[0] = x_ref[0] + gate_ref[0] * y


def _merge_call(o_f, o_b, p, b_lat, x, mod3, a_norm_w, wpa, wpb, wo):
    b, t, d = x.shape
    tm = MERGE_T
    return pl.pallas_call(
        _merge_kernel,
        grid=(b, t // tm),
        in_specs=[pl.BlockSpec((1, 1, tm, 1024), lambda bi, i: (0, bi, i, 0)),
                  pl.BlockSpec((1, 1, tm, 1024), lambda bi, i: (0, bi, i, 0)),
                  pl.BlockSpec((1, tm, 1024), lambda bi, i: (bi, i, COL_AZ // 1024)),
                  pl.BlockSpec((1, tm, 1024), lambda bi, i: (bi, i, 0)),
                  pl.BlockSpec((1, tm, d), lambda bi, i: (bi, i, COL_BRA // d)),
                  pl.BlockSpec((1, tm, d), lambda bi, i: (bi, i, COL_BRB // d)),
                  pl.BlockSpec((1, tm, d), lambda bi, i: (bi, i, 0)),
                  pl.BlockSpec((1, 1, d), lambda bi, i: (bi, 0, 2)),
                  pl.BlockSpec((1, HD), lambda bi, i: (0, 0)),
                  pl.BlockSpec((1024, d), lambda bi, i: (0, 0)),
                  pl.BlockSpec((1024, d), lambda bi, i: (0, 0)),
                  pl.BlockSpec((d, d), lambda bi, i: (0, 0))],
        out_specs=pl.BlockSpec((1, tm, d), lambda bi, i: (bi, i, 0)),
        out_shape=jax.ShapeDtypeStruct((b, t, d), F32),
        scratch_shapes=[pltpu.VMEM((tm, 1024), BF16)],
        compiler_params=pltpu.CompilerParams(dimension_semantics=("arbitrary", "arbitrary"),
                                             vmem_limit_bytes=VMEM_LIMIT),
        name="merge",
    )(o_f, o_b, p, b_lat, p, p, x, mod3, a_norm_w, wpa, wpb, wo)


def _rope_tables():
    t = jnp.arange(SEQ)
    row = (t // GRID_W).astype(F32)
    col = (t % GRID_W).astype(F32)
    n_freq = HD // 4
    inv = ROPE_THETA ** (-jnp.arange(n_freq, dtype=F32) / n_freq)
    ang = jnp.concatenate([row[:, None] * inv, col[:, None] * inv], axis=-1)
    cos, sin = jnp.cos(ang), jnp.sin(ang)
    cosf = jnp.concatenate([cos, cos], axis=-1)
    sinf = jnp.concatenate([-sin, sin], axis=-1)
    cosf = jnp.concatenate([cosf, jnp.ones((CTX_LEN, HD), F32)], axis=0)
    sinf = jnp.concatenate([sinf, jnp.zeros((CTX_LEN, HD), F32)], axis=0)
    return cosf, sinf


def _gate_row(a):
    return jnp.zeros((1, 128), F32).at[0, 2 * HEADS:4 * HEADS].set(a.reshape(-1).astype(F32))


def _layer(x, ctx, c, c_ctx, w_mod, b_mod, norm_w, w_in, conv_w, a_log, dt_bias,
           a_norm_w, q_norm_w, k_norm_w, w_proj_a, w_proj_b, w_out):
    b = x.shape[0]
    d = D_MODEL
    w_all = jnp.concatenate(
        [w_in[:, 0:4096], w_in[:, 4128:5152], w_in[:, 5664:6688], w_in[:, 6688:10784],
         w_in[:, 5152:5664], w_in[:, 4096:4128],
         jnp.zeros((d, P_COLS - COL_SMALL - 544), w_in.dtype)], axis=1).astype(BF16)
    wpa = w_proj_a.astype(BF16)
    wpb = w_proj_b.astype(BF16)
    wo = w_out.astype(BF16)

    cc = jnp.concatenate([c, c_ctx[None, :], jnp.zeros((8 - b - 1, d), F32)], axis=0)
    mod = _mod_call(cc, w_mod, b_mod[None, :])
    mod3 = mod.reshape(8, 1, 3 * d)

    nw = norm_w[None, :]
    p = _inproj_latent(x, mod3, nw, w_all)
    p = _inproj_ctx(ctx, mod3, nw, w_all, p)

    cosf, sinf = _rope_tables()
    g, bg, kn = _prep_call(p, conv_w, _gate_row(a_log), _gate_row(dt_bias), k_norm_w[None, :], cosf, sinf)
    o_f, o_b = _gdn_call(g, bg)
    b_lat = _attn_call(p, kn, q_norm_w[None, :], cosf, sinf)
    return _merge_call(o_f, o_b, p, b_lat, x, mod3, a_norm_w[None, :], wpa, wpb, wo)


def kernel(x, c, ctx, c_ctx, w_mod, b_mod, norm_w, w_in, conv_w, a_log, dt_bias, a_norm_w,
           q_norm_w, k_norm_w, w_proj_a, w_proj_b, w_out):
    assert w_mod.shape[0] == 1
    return _layer(x, ctx, c, c_ctx, w_mod[0], b_mod[0], norm_w[0], w_in[0], conv_w[0], a_log[0],
                  dt_bias[0], a_norm_w[0], q_norm_w[0], k_norm_w[0], w_proj_a[0], w_proj_b[0],
                  w_out[0])
```

```python
import functools
import math

import jax
import jax.numpy as jnp
from jax import lax
from jax.experimental import pallas as pl
from jax.experimental.pallas import tpu as pltpu

F32 = jnp.float32
BF16 = jnp.bfloat16

D_MODEL = 2048
SEQ = 2048
CTX_LEN = 256
GRID_W = 64
EPS = 1e-6
HEADS = 8
HD = 128
KV_HEADS = 2
A_CONV = 5
ROPE_THETA = 10000.0
CHUNK = 64

COL_AQ, COL_AK, COL_AV, COL_AZ, COL_BQ, COL_BZ, COL_BRA, COL_BRB, COL_SMALL = (
    0, 1024, 2048, 3072, 4096, 5120, 6144, 8192, 10240)
P_COLS = 11264
SMALL_BK, SMALL_BV, SMALL_BA = 0, 256, 512
CTX_COLS = 4096
CTX_COL_SMALL = 3072

VMEM_LIMIT = 56 * 1024 * 1024
LANES = 128


def _silu(x):
    return x * jax.nn.sigmoid(x)


def _softplus(x):
    return jnp.maximum(x, 0.0) + jnp.log1p(jnp.exp(-jnp.abs(x)))


def _bdot(a, b):
    return jnp.dot(a.astype(BF16), b.astype(BF16), preferred_element_type=F32)


def _bdot_nt(a, b):
    return lax.dot_general(a.astype(BF16), b.astype(BF16), (((1,), (1,)), ((), ())),
                           preferred_element_type=F32)


def _bdot_tn(a, b):
    return lax.dot_general(a.astype(BF16), b.astype(BF16), (((0,), (0,)), ((), ())),
                           preferred_element_type=F32)


def _params(n_axes):
    return pltpu.CompilerParams(dimension_semantics=("arbitrary",) * n_axes,
                                vmem_limit_bytes=VMEM_LIMIT)


WP_BLK = 512
WP_ROWS = 512
WP_SHIFT = 32
WP_NBLK = P_COLS // WP_BLK


def _wprep_src(n):
    return jnp.where(n < 10, n, jnp.where(n < 20, n + 1, jnp.where(n == 20, 10, 8)))


def _wprep_kernel(main_ref, extra_ref, o_ref):
    n = pl.program_id(0)
    a = main_ref[...]

    @pl.when(n < 8)
    def _():
        o_ref[...] = a.astype(BF16)

    @pl.when(jnp.logical_and(n >= 8, n < WP_NBLK - 1))
    def _():
        e = extra_ref[...]
        ra = pltpu.roll(a, WP_BLK - WP_SHIFT, 1)
        re = pltpu.roll(e, LANES - WP_SHIFT, 1)
        lane = lax.broadcasted_iota(jnp.int32, e.shape, 1)
        tail = jnp.where(lane < LANES - WP_SHIFT, ra[:, WP_BLK - LANES:], re)
        o_ref[:, :WP_BLK - LANES] = ra[:, :WP_BLK - LANES].astype(BF16)
        o_ref[:, WP_BLK - LANES:] = tail.astype(BF16)

    @pl.when(n == WP_NBLK - 1)
    def _():
        lane = lax.broadcasted_iota(jnp.int32, a.shape, 1)
        o_ref[...] = jnp.where(lane < WP_SHIFT, a, 0.0).astype(BF16)


def _wprep_call(w_in):
    d = w_in.shape[0]
    ratio = WP_BLK // LANES
    return pl.pallas_call(
        _wprep_kernel,
        grid=(WP_NBLK, d // WP_ROWS),
        in_specs=[pl.BlockSpec((WP_ROWS, WP_BLK), lambda n, r: (r, _wprep_src(n))),
                  pl.BlockSpec((WP_ROWS, LANES), lambda n, r: (r, ratio * _wprep_src(n) + ratio))],
        out_specs=pl.BlockSpec((WP_ROWS, WP_BLK), lambda n, r: (r, n)),
        out_shape=jax.ShapeDtypeStruct((d, P_COLS), BF16),
        compiler_params=_params(2),
        name="wprep",
    )(w_in, w_in)


def _mod_kernel(c_ref, w_ref, b_ref, o_ref):
    o_ref[...] = _bdot(_silu(c_ref[...]), w_ref[...]) + b_ref[...]


def _mod_call(cc, w_mod, b_mod):
    tn = 768
    n = w_mod.shape[1]
    return pl.pallas_call(
        _mod_kernel,
        grid=(n // tn,),
        in_specs=[pl.BlockSpec((8, D_MODEL), lambda j: (0, 0)),
                  pl.BlockSpec((D_MODEL, tn), lambda j: (0, j)),
                  pl.BlockSpec((1, tn), lambda j: (0, j))],
        out_specs=pl.BlockSpec((8, tn), lambda j: (0, j)),
        out_shape=jax.ShapeDtypeStruct((8, n), F32),
        compiler_params=_params(1),
        name="mod",
    )(cc, w_mod, b_mod)


def _inproj_kernel(x_ref, shift_ref, scale_ref, nw_ref, w_ref, o_ref, h_ref):
    @pl.when(pl.program_id(2) == 0)
    def _():
        x = x_ref[0]
        r = lax.rsqrt(jnp.mean(x * x, axis=-1, keepdims=True) + EPS)
        h = (x * r) * nw_ref[...] * (1.0 + scale_ref[0]) + shift_ref[0]
        h_ref[...] = h.astype(BF16)

    o_ref[0] = jnp.dot(h_ref[...], w_ref[...], preferred_element_type=F32).astype(BF16)


def _inproj_call(x, mod3, norm_w, w_all, *, mod_row, w_col, n_col, name):
    nb, t, d = x.shape
    tm, tn = 1024, 1024
    return pl.pallas_call(
        _inproj_kernel,
        grid=(nb, t // tm, n_col),
        in_specs=[pl.BlockSpec((1, tm, d), lambda bi, i, j: (bi, i, 0)),
                  pl.BlockSpec((1, 1, d), lambda bi, i, j: (mod_row(bi), 0, 0)),
                  pl.BlockSpec((1, 1, d), lambda bi, i, j: (mod_row(bi), 0, 1)),
                  pl.BlockSpec((1, d), lambda bi, i, j: (0, 0)),
                  pl.BlockSpec((d, tn), lambda bi, i, j: (0, w_col(j)))],
        out_specs=pl.BlockSpec((1, tm, tn), lambda bi, i, j: (bi, i, j)),
        out_shape=jax.ShapeDtypeStruct((nb, t, n_col * tn), BF16),
        scratch_shapes=[pltpu.VMEM((tm, d), BF16)],
        compiler_params=_params(3),
        name=name,
    )(x, mod3, mod3, norm_w, w_all)


PREP_T = 256
HALO = 16


CONV_OFFS = (-2, -1, 1, 2)


def _prep_kernel(*refs, latent):
    if latent:
        (main_ref, prev_ref, next_ref, small_ref, shift_ref, convw_ref, alog_ref, dtb_ref, knw_ref,
         cos_ref, sin_ref, g_ref, bg_ref, kn_ref) = refs
        i = pl.program_id(1)
        has_prev = i > 0
        has_next = i < pl.num_programs(1) - 1
        row8 = lax.broadcasted_iota(jnp.int32, (8, 1024), 0)
    else:
        (main_ref, small_ref, shift_ref, convw_ref, alog_ref, dtb_ref, knw_ref,
         g_ref, bg_ref, kn_ref) = refs
    t = PREP_T
    for c in range(3):
        cs = slice(c * 1024, (c + 1) * 1024)
        w = [convw_ref[k:k + 1, cs] for k in range(A_CONV)]
        x = main_ref[0, :, cs]
        y = jnp.dot(shift_ref[...], x, preferred_element_type=F32)
        acc = x.astype(F32) * w[2]
        for n, k in enumerate((0, 1, 3, 4)):
            acc = acc + y[n * t:(n + 1) * t] * w[k]
        if latent:
            pv = jnp.where(has_prev, prev_ref[0, HALO - 8:, cs].astype(F32), 0.0)
            nx = jnp.where(has_next, next_ref[0, :8, cs].astype(F32), 0.0)
            top = (jnp.where(row8 < 2, pltpu.roll(pv, 2, 0), 0.0) * w[0]
                   + jnp.where(row8 < 1, pltpu.roll(pv, 1, 0), 0.0) * w[1])
            bot = (jnp.where(row8 >= 7, pltpu.roll(nx, 7, 0), 0.0) * w[3]
                   + jnp.where(row8 >= 6, pltpu.roll(nx, 6, 0), 0.0) * w[4])
            acc = jnp.concatenate([acc[:8] + top, acc[8:t - 8], acc[t - 8:] + bot], axis=0)
        act = _silu(acc)
        if c == 2:
            g_ref[0, :, cs] = act.astype(BF16)
        else:
            mul = HD ** -0.5 if c == 0 else 1.0
            for h in range(HEADS):
                a = act[:, h * HD:(h + 1) * HD]
                r = lax.rsqrt(jnp.sum(a * a, axis=-1, keepdims=True) + EPS)
                g_ref[0, :, c * 1024 + h * HD:c * 1024 + (h + 1) * HD] = (a * (r * mul)).astype(BF16)

    ba = small_ref[0, :, SMALL_BA:SMALL_BA + LANES].astype(F32)
    beta = jax.nn.sigmoid(ba)
    gdec = -jnp.exp(alog_ref[...]) * _softplus(ba + dtb_ref[...])
    lane = lax.broadcasted_iota(jnp.int32, ba.shape, 1)
    bg_ref[0] = jnp.where(lane < 2 * HEADS, beta, gdec)

    for h in range(KV_HEADS):
        kh = small_ref[0, :, SMALL_BK + h * HD:SMALL_BK + (h + 1) * HD].astype(F32)
        r = lax.rsqrt(jnp.mean(kh * kh, axis=-1, keepdims=True) + EPS)
        kn = kh * r * knw_ref[...]
        if latent:
            kn = kn * cos_ref[...] + pltpu.roll(kn, HD // 2, 1) * sin_ref[...]
        kn_ref[0, :, h * HD:(h + 1) * HD] = kn.astype(BF16)


def _prep_call(p, conv_w, alog_row, dtb_row, k_norm_w, cosf, sinf, *, latent):
    b, t, _ = p.shape
    nt = t // PREP_T
    const = lambda bi, i: (0, 0)
    tile = lambda bi, i: (bi, i, 0)
    if latent:
        hb = PREP_T // HALO
        last_hb = t // HALO - 1
        small_col = COL_SMALL // 1024
        in_specs = [pl.BlockSpec((1, PREP_T, 3072), tile),
                    pl.BlockSpec((1, HALO, 3072), lambda bi, i: (bi, jnp.maximum(i * hb - 1, 0), 0)),
                    pl.BlockSpec((1, HALO, 3072), lambda bi, i: (bi, jnp.minimum((i + 1) * hb, last_hb), 0)),
                    pl.BlockSpec((1, PREP_T, 1024), lambda bi, i: (bi, i, small_col))]
        args = [p, p, p, p]
    else:
        small_col = CTX_COL_SMALL // 1024
        in_specs = [pl.BlockSpec((1, PREP_T, 3072), tile),
                    pl.BlockSpec((1, PREP_T, 1024), lambda bi, i: (bi, i, small_col))]
        args = [p, p]
    shifts = jnp.concatenate([jnp.eye(PREP_T, k=off, dtype=BF16) for off in CONV_OFFS], axis=0)
    in_specs += [pl.BlockSpec(shifts.shape, const),
                 pl.BlockSpec((A_CONV, 3072), const), pl.BlockSpec((1, LANES), const),
                 pl.BlockSpec((1, LANES), const), pl.BlockSpec((1, HD), const)]
    args += [shifts, conv_w, alog_row, dtb_row, k_norm_w]
    if latent:
        in_specs += [pl.BlockSpec((PREP_T, HD), lambda bi, i: (i, 0))] * 2
        args += [cosf, sinf]
    return pl.pallas_call(
        functools.partial(_prep_kernel, latent=latent),
        grid=(b, nt),
        in_specs=in_specs,
        out_specs=[pl.BlockSpec((1, PREP_T, 3072), tile),
                   pl.BlockSpec((1, PREP_T, LANES), tile),
                   pl.BlockSpec((1, PREP_T, KV_HEADS * HD), tile)],
        out_shape=[jax.ShapeDtypeStruct((b, t, 3072), BF16),
                   jax.ShapeDtypeStruct((b, t, LANES), F32),
                   jax.ShapeDtypeStruct((b, t, KV_HEADS * HD), BF16)],
        compiler_params=_params(2),
        name="prep_latent" if latent else "prep_ctx",
    )(*args)


def _chunk_masks(direction):
    c = CHUNK
    ii = lax.broadcasted_iota(jnp.int32, (c, c), 0)
    jj = lax.broadcasted_iota(jnp.int32, (c, c), 1)
    if direction == 0:
        return ii >= jj, ii > jj, c - 1
    return ii <= jj, ii < jj, 0


def _chunk_cumsum(bg, incl):
    tri = incl.astype(BF16)
    g1 = bg.astype(BF16)
    r1 = bg - g1.astype(F32)
    g2 = r1.astype(BF16)
    g3 = (r1 - g2.astype(F32)).astype(BF16)
    return (jnp.dot(tri, g1, preferred_element_type=F32) + jnp.dot(tri, g2, preferred_element_type=F32)
            + jnp.dot(tri, g3, preferred_element_type=F32))


def _gdn_kernel(*refs, latent):
    step = pl.program_id(1)
    if latent:
        qkv_f_ref, bg_f_ref, qkv_b_ref, bg_b_ref, s0_f_ref, s0_b_ref, o_f_ref, o_b_ref, s_f_ref, s_b_ref = refs

        @pl.when(step == 0)
        def _():
            s_f_ref[...] = s0_f_ref[...]
            s_b_ref[...] = s0_b_ref[...]

        o_refs = (o_f_ref, o_b_ref)
    else:
        qkv_f_ref, bg_f_ref, qkv_b_ref, bg_b_ref, s_f_ref, s_b_ref = refs

        @pl.when(step == 0)
        def _():
            s_f_ref[...] = jnp.zeros(s_f_ref.shape, F32)
            s_b_ref[...] = jnp.zeros(s_b_ref.shape, F32)

    state = lambda n, d, h: (s_f_ref, s_b_ref)[d].at[n, h]
    c = CHUNK
    qkv_refs = (qkv_f_ref, qkv_b_ref)
    rows = range(qkv_f_ref.shape[0])
    chains = [(n, d, h) for n in rows for d in range(2) for h in range(HEADS)]
    ii = lax.broadcasted_iota(jnp.int32, (c, c), 0)
    jj = lax.broadcasted_iota(jnp.int32, (c, c), 1)
    eye = (ii == jj).astype(F32)

    masks = [_chunk_masks(d) for d in range(2)]
    bg_refs = (bg_f_ref, bg_b_ref)
    bgs = {(n, d): bg_refs[d][n] for n in rows for d in range(2)}
    gcs = {nd: _chunk_cumsum(bgs[nd], masks[nd[1]][0]) for nd in bgs}
    gcts = {nd: jnp.concatenate([gcs[nd], jnp.zeros((LANES - c, LANES), F32)], axis=0).T for nd in bgs}

    q, k, v, kf, beta, gcol, glast, decay = {}, {}, {}, {}, {}, {}, {}, {}
    for ch in chains:
        n, d, h = ch
        incl, _, last = masks[d]
        cb = d * HEADS + h
        cg = 2 * HEADS + cb
        q[ch] = qkv_refs[d][n, :, h * HD:(h + 1) * HD]
        k[ch] = qkv_refs[d][n, :, 1024 + h * HD:1024 + (h + 1) * HD]
        v[ch] = qkv_refs[d][n, :, 2048 + h * HD:2048 + (h + 1) * HD]
        kf[ch] = k[ch].astype(F32)
        beta[ch] = bgs[n, d][:, cb:cb + 1]
        gcol[ch] = gcs[n, d][:, cg:cg + 1]
        grow = gcts[n, d][cg:cg + 1, 0:c]
        glast[ch] = gcs[n, d][last:last + 1, cg:cg + 1]
        decay[ch] = jnp.where(incl, jnp.exp(jnp.where(incl, gcol[ch] - grow, 0.0)), 0.0)

    kk = {ch: _bdot_nt(k[ch], k[ch]) for ch in chains}
    l_mat = {ch: beta[ch] * kk[ch] * jnp.where(masks[ch[1]][1], decay[ch], 0.0) for ch in chains}
    if latent:
        qk = {ch: _bdot_nt(q[ch], k[ch]) for ch in chains}
        attn = {ch: qk[ch] * decay[ch] for ch in chains}

    lp = dict(l_mat)
    t_inv = {ch: eye - l_mat[ch] for ch in chains}
    for _ in range(c.bit_length() - 2):
        lp = {ch: _bdot(lp[ch], lp[ch]) for ch in chains}
        t_inv = {ch: t_inv[ch] + _bdot(t_inv[ch], lp[ch]) for ch in chains}

    eg = {ch: jnp.exp(gcol[ch]) for ch in chains}
    rhs = {ch: jnp.concatenate([v[ch].astype(F32) * beta[ch], kf[ch] * (beta[ch] * eg[ch])], axis=1)
           for ch in chains}
    uw = {ch: _bdot(t_inv[ch], rhs[ch]) for ch in chains}
    s = {ch: state(*ch)[...] for ch in chains}
    if latent:
        lhs = {ch: jnp.concatenate([uw[ch][:, HD:], q[ch].astype(F32) * eg[ch]], axis=0) for ch in chains}
    else:
        lhs = {ch: uw[ch][:, HD:] for ch in chains}
    ws_qs = {ch: _bdot(lhs[ch], s[ch]) for ch in chains}
    v_new = {ch: uw[ch][:, :HD] - ws_qs[ch][:c] for ch in chains}
    k_dec = {ch: kf[ch] * jnp.exp(glast[ch] - gcol[ch]) for ch in chains}
    kv = {ch: _bdot_tn(k_dec[ch], v_new[ch]) for ch in chains}
    if latent:
        av = {ch: _bdot(attn[ch], v_new[ch]) for ch in chains}
    for ch in chains:
        n, d, h = ch
        if latent:
            o_refs[d][n, :, h * HD:(h + 1) * HD] = (ws_qs[ch][c:] + av[ch]).astype(BF16)
        state(*ch)[...] = s[ch] * jnp.exp(glast[ch]) + kv[ch]


GDN_ROWS = 4


def _gdn_call(g, bg, states=None):
    b, t, _ = g.shape
    nc = t // CHUNK
    nr = GDN_ROWS
    latent = states is not None
    fwd = lambda bi, s: (bi, s, 0)
    bwd = lambda bi, s: (bi, nc - 1 - s, 0)
    in_specs = [pl.BlockSpec((nr, CHUNK, 3072), fwd), pl.BlockSpec((nr, CHUNK, LANES), fwd),
                pl.BlockSpec((nr, CHUNK, 3072), bwd), pl.BlockSpec((nr, CHUNK, LANES), bwd)]
    state_spec = pl.BlockSpec((nr, HEADS, HD, HD), lambda bi, s: (bi, 0, 0, 0))
    state_shape = jax.ShapeDtypeStruct((b, HEADS, HD, HD), F32)
    if latent:
        return pl.pallas_call(
            functools.partial(_gdn_kernel, latent=True),
            grid=(b // nr, nc),
            in_specs=in_specs + [state_spec, state_spec],
            out_specs=[pl.BlockSpec((nr, CHUNK, HEADS * HD), fwd),
                       pl.BlockSpec((nr, CHUNK, HEADS * HD), bwd)],
            out_shape=[jax.ShapeDtypeStruct((b, t, HEADS * HD), BF16)] * 2,
            scratch_shapes=[pltpu.VMEM((nr, HEADS, HD, HD), F32)] * 2,
            compiler_params=_params(2),
            name="gdn_latent",
        )(g, bg, g, bg, *states)
    return pl.pallas_call(
        functools.partial(_gdn_kernel, latent=False),
        grid=(b // nr, nc),
        in_specs=in_specs,
        out_specs=[state_spec, state_spec],
        out_shape=[state_shape, state_shape],
        compiler_params=_params(2),
        name="gdn_ctx",
    )(g, bg, g, bg)


ATT_TQ = 256
Q_PER_KV = HEADS // KV_HEADS
Q_SCALE = HD ** -0.5 * math.log2(math.e)


def _attn_kernel(q_ref, z_ref, kl_ref, kc_ref, vl_ref, vc_ref, qnw_ref, cos_ref, sin_ref, o_ref):
    kl, kc, vl, vc = kl_ref[0], kc_ref[0], vl_ref[0], vc_ref[0]

    def scores(j):
        qh = q_ref[0, :, j * HD:(j + 1) * HD].astype(F32)
        r = lax.rsqrt(jnp.mean(qh * qh, axis=-1, keepdims=True) + EPS)
        qn = qh * r * qnw_ref[...]
        qr = (qn * cos_ref[...] + pltpu.roll(qn, HD // 2, 1) * sin_ref[...]) * Q_SCALE
        return _bdot_nt(qr, kl), _bdot_nt(qr, kc)

    nxt = scores(0)
    for j in range(Q_PER_KV):
        sl = slice(j * HD, (j + 1) * HD)
        s_l, s_c = nxt
        if j + 1 < Q_PER_KV:
            nxt = scores(j + 1)
        m = jnp.maximum(jnp.max(s_l, axis=-1, keepdims=True), jnp.max(s_c, axis=-1, keepdims=True))
        e_l = jnp.exp2(s_l - m)
        e_c = jnp.exp2(s_c - m)
        den = jnp.sum(e_l, axis=-1, keepdims=True) + jnp.sum(e_c, axis=-1, keepdims=True)
        o = (jnp.dot(e_l.astype(BF16), vl, preferred_element_type=F32)
             + jnp.dot(e_c.astype(BF16), vc, preferred_element_type=F32)) / den
        o_ref[0, :, sl] = (o * _silu(z_ref[0, :, sl].astype(F32))).astype(BF16)


def _attn_call(p, p_ctx, kn, kn_ctx, q_norm_w, cosf, sinf):
    b = p.shape[0]
    qw = Q_PER_KV * HD
    lat_v = (COL_SMALL + SMALL_BV) // HD
    ctx_v = (CTX_COL_SMALL + SMALL_BV) // HD
    return pl.pallas_call(
        _attn_kernel,
        grid=(b, KV_HEADS, SEQ // ATT_TQ),
        in_specs=[pl.BlockSpec((1, ATT_TQ, qw), lambda bi, g, i: (bi, i, COL_BQ // qw + g)),
                  pl.BlockSpec((1, ATT_TQ, qw), lambda bi, g, i: (bi, i, COL_BZ // qw + g)),
                  pl.BlockSpec((1, SEQ, HD), lambda bi, g, i: (bi, 0, g)),
                  pl.BlockSpec((1, CTX_LEN, HD), lambda bi, g, i: (bi, 0, g)),
                  pl.BlockSpec((1, SEQ, HD), lambda bi, g, i: (bi, 0, lat_v + g)),
                  pl.BlockSpec((1, CTX_LEN, HD), lambda bi, g, i: (bi, 0, ctx_v + g)),
                  pl.BlockSpec((1, HD), lambda bi, g, i: (0, 0)),
                  pl.BlockSpec((ATT_TQ, HD), lambda bi, g, i: (i, 0)),
                  pl.BlockSpec((ATT_TQ, HD), lambda bi, g, i: (i, 0))],
        out_specs=pl.BlockSpec((1, ATT_TQ, qw), lambda bi, g, i: (bi, i, g)),
        out_shape=jax.ShapeDtypeStruct((b, SEQ, HEADS * HD), BF16),
        compiler_params=_params(3),
        name="attn",
    )(p, p, kn, kn_ctx, p, p_ctx, q_norm_w, cosf, sinf)


MERGE_T = 256


def _merge_kernel(of_ref, ob_ref, az_ref, bl_ref, bra_ref, brb_ref, x_ref, gate_ref, anw_ref,
                  wpa_ref, wpb_ref, wo_ref, out_ref, a_ref):
    o = of_ref[0].astype(F32) + ob_ref[0].astype(F32)
    for h in range(HEADS):
        sl = slice(h * HD, (h + 1) * HD)
        oh = o[:, sl]
        r = lax.rsqrt(jnp.mean(oh * oh, axis=-1, keepdims=True) + EPS)
        a_ref[:, sl] = (oh * r * anw_ref[...] * _silu(az_ref[0, :, sl].astype(F32))).astype(BF16)
    pa = jnp.dot(a_ref[...], wpa_ref[...], preferred_element_type=F32)
    pb = jnp.dot(bl_ref[0], wpb_ref[...], preferred_element_type=F32)
    m = (jax.nn.sigmoid(bra_ref[0].astype(F32)) * pa + jax.nn.sigmoid(brb_ref[0].astype(F32)) * pb)
    y = jnp.dot(m.astype(BF16), wo_ref[...], preferred_element_type=F32)
    out_ref[0] = x_ref[0] + gate_ref[0] * y


def _merge_call(o_f, o_b, p, b_lat, x, mod3, a_norm_w, wpa, wpb, wo):
    b, t, d = x.shape
    tm = MERGE_T
    tile = lambda bi, i: (bi, i, 0)
    const = lambda bi, i: (0, 0)
    return pl.pallas_call(
        _merge_kernel,
        grid=(b, t // tm),
        in_specs=[pl.BlockSpec((1, tm, 1024), tile),
                  pl.BlockSpec((1, tm, 1024), tile),
                  pl.BlockSpec((1, tm, 1024), lambda bi, i: (bi, i, COL_AZ // 1024)),
                  pl.BlockSpec((1, tm, 1024), tile),
                  pl.BlockSpec((1, tm, d), lambda bi, i: (bi, i, COL_BRA // d)),
                  pl.BlockSpec((1, tm, d), lambda bi, i: (bi, i, COL_BRB // d)),
                  pl.BlockSpec((1, tm, d), tile),
                  pl.BlockSpec((1, 1, d), lambda bi, i: (bi, 0, 2)),
                  pl.BlockSpec((1, HD), const),
                  pl.BlockSpec((1024, d), const),
                  pl.BlockSpec((1024, d), const),
                  pl.BlockSpec((d, d), const)],
        out_specs=pl.BlockSpec((1, tm, d), tile),
        out_shape=jax.ShapeDtypeStruct((b, t, d), F32),
        scratch_shapes=[pltpu.VMEM((tm, 1024), BF16)],
        compiler_params=_params(2),
        name="merge",
    )(o_f, o_b, p, b_lat, p, p, x, mod3, a_norm_w, wpa, wpb, wo)


def _rope_tables():
    t = jnp.arange(SEQ)
    row = (t // GRID_W).astype(F32)
    col = (t % GRID_W).astype(F32)
    n_freq = HD // 4
    inv = ROPE_THETA ** (-jnp.arange(n_freq, dtype=F32) / n_freq)
    ang = jnp.concatenate([row[:, None] * inv, col[:, None] * inv], axis=-1)
    cos, sin = jnp.cos(ang), jnp.sin(ang)
    return jnp.concatenate([cos, cos], axis=-1), jnp.concatenate([-sin, sin], axis=-1)


def _gate_row(a):
    return jnp.zeros((1, LANES), F32).at[0, 2 * HEADS:4 * HEADS].set(a.reshape(-1).astype(F32))


def _layer(x, ctx, c, c_ctx, w_mod, b_mod, norm_w, w_in, conv_w, a_log, dt_bias,
           a_norm_w, q_norm_w, k_norm_w, w_proj_a, w_proj_b, w_out):
    b = x.shape[0]
    d = D_MODEL
    w_all = _wprep_call(w_in)
    wpa = w_proj_a.astype(BF16)
    wpb = w_proj_b.astype(BF16)
    wo = w_out.astype(BF16)

    cc = jnp.concatenate([c, c_ctx[None, :], jnp.zeros((8 - b - 1, d), F32)], axis=0)
    mod3 = _mod_call(cc, w_mod, b_mod[None, :]).reshape(8, 1, 3 * d)

    nw = norm_w[None, :]
    p = _inproj_call(x, mod3, nw, w_all, mod_row=lambda bi: bi, w_col=lambda j: j,
                     n_col=P_COLS // 1024, name="inproj_latent")
    small_blk = COL_SMALL // 1024
    p_ctx = _inproj_call(ctx.reshape(1, b * CTX_LEN, d), mod3, nw, w_all, mod_row=lambda bi: b,
                         w_col=lambda j: jnp.where(j < 3, j, small_blk), n_col=CTX_COLS // 1024,
                         name="inproj_ctx").reshape(b, CTX_LEN, CTX_COLS)

    cosf, sinf = _rope_tables()
    alog_row, dtb_row, knw = _gate_row(a_log), _gate_row(dt_bias), k_norm_w[None, :]
    g, bg, kn = _prep_call(p, conv_w, alog_row, dtb_row, knw, cosf, sinf, latent=True)
    g_ctx, bg_ctx, kn_ctx = _prep_call(p_ctx, conv_w, alog_row, dtb_row, knw, None, None, latent=False)
    states = _gdn_call(g_ctx, bg_ctx)
    o_f, o_b = _gdn_call(g, bg, states)
    b_lat = _attn_call(p, p_ctx, kn, kn_ctx, q_norm_w[None, :], cosf, sinf)
    return _merge_call(o_f, o_b, p, b_lat, x, mod3, a_norm_w[None, :], wpa, wpb, wo)


def kernel(x, c, ctx, c_ctx, w_mod, b_mod, norm_w, w_in, conv_w, a_log, dt_bias, a_norm_w,
           q_norm_w, k_norm_w, w_proj_a, w_proj_b, w_out):
    assert w_mod.shape[0] == 1
    return _layer(x, ctx, c, c_ctx, w_mod[0], b_mod[0], norm_w[0], w_in[0], conv_w[0], a_log[0],
                  dt_bias[0], a_norm_w[0], q_norm_w[0], k_norm_w[0], w_proj_a[0], w_proj_b[0],
                  w_out[0])
```

```python
import functools
import math

import jax
import jax.numpy as jnp
from jax import lax
from jax.experimental import pallas as pl
from jax.experimental.pallas import tpu as pltpu

F32 = jnp.float32
BF16 = jnp.bfloat16

D_MODEL = 2048
SEQ = 2048
CTX_LEN = 256
GRID_W = 64
EPS = 1e-6
HEADS = 8
HD = 128
KV_HEADS = 2
A_CONV = 5
ROPE_THETA = 10000.0
CHUNK = 64

COL_AQ, COL_AK, COL_AV, COL_AZ, COL_BQ, COL_BZ, COL_BRA, COL_BRB, COL_SMALL = (
    0, 1024, 2048, 3072, 4096, 5120, 6144, 8192, 10240)
P_COLS = 11264
SMALL_BK, SMALL_BV, SMALL_BA = 0, 256, 512
CTX_COLS = 4096
CTX_COL_SMALL = 3072

VMEM_LIMIT = 56 * 1024 * 1024
LANES = 128
INPROJ_TN = 1024


def _silu(x):
    return x * jax.nn.sigmoid(x)


def _softplus(x):
    return jnp.maximum(x, 0.0) + jnp.log1p(jnp.exp(-jnp.abs(x)))


def _bdot(a, b):
    return jnp.dot(a.astype(BF16), b.astype(BF16), preferred_element_type=F32)


def _bdot_nt(a, b):
    return lax.dot_general(a.astype(BF16), b.astype(BF16), (((1,), (1,)), ((), ())),
                           preferred_element_type=F32)


def _bdot_tn(a, b):
    return lax.dot_general(a.astype(BF16), b.astype(BF16), (((0,), (0,)), ((), ())),
                           preferred_element_type=F32)


def _params(n_axes):
    return pltpu.CompilerParams(dimension_semantics=("arbitrary",) * n_axes,
                                vmem_limit_bytes=VMEM_LIMIT)


WP_ROWS = 512
WP_SHIFT = 32
WP_NBLK = P_COLS // WP_ROWS


def _wprep_src(n):
    return jnp.where(n < 10, n, jnp.where(n < 20, n + 1, jnp.where(n == 20, 10, 8)))


def _wprep_kernel(main_ref, extra_ref, o_ref):
    n = pl.program_id(0)
    keep = WP_ROWS - WP_SHIFT

    @pl.when(n < 8)
    def _():
        o_ref[...] = main_ref[...].astype(BF16)

    @pl.when(jnp.logical_and(n >= 8, n < WP_NBLK - 1))
    def _():
        o_ref[:keep] = main_ref[WP_SHIFT:].astype(BF16)
        o_ref[keep:] = extra_ref[...].astype(BF16)

    @pl.when(n == WP_NBLK - 1)
    def _():
        o_ref[:WP_SHIFT] = main_ref[:WP_SHIFT].astype(BF16)
        o_ref[WP_SHIFT:] = jnp.zeros((keep, o_ref.shape[1]), BF16)


def _wprep_call(w_t):
    d = w_t.shape[1]
    ratio = WP_ROWS // WP_SHIFT
    return pl.pallas_call(
        _wprep_kernel,
        grid=(WP_NBLK,),
        in_specs=[pl.BlockSpec((WP_ROWS, d), lambda n: (_wprep_src(n), 0)),
                  pl.BlockSpec((WP_SHIFT, d), lambda n: (ratio * _wprep_src(n) + ratio, 0))],
        out_specs=pl.BlockSpec((WP_ROWS, d), lambda n: (n, 0)),
        out_shape=jax.ShapeDtypeStruct((P_COLS, d), BF16),
        compiler_params=_params(1),
        name="wprep",
    )(w_t, w_t)


def _mod_kernel(c_ref, w_ref, b_ref, o_ref):
    o_ref[...] = _bdot(_silu(c_ref[...]), w_ref[...]) + b_ref[...]


def _mod_call(cc, w_mod, b_mod):
    tn = 768
    n = w_mod.shape[1]
    return pl.pallas_call(
        _mod_kernel,
        grid=(n // tn,),
        in_specs=[pl.BlockSpec((8, D_MODEL), lambda j: (0, 0)),
                  pl.BlockSpec((D_MODEL, tn), lambda j: (0, j)),
                  pl.BlockSpec((1, tn), lambda j: (0, j))],
        out_specs=pl.BlockSpec((8, tn), lambda j: (0, j)),
        out_shape=jax.ShapeDtypeStruct((8, n), F32),
        compiler_params=_params(1),
        name="mod",
    )(cc, w_mod, b_mod)


def _inproj_kernel(x_ref, shift_ref, scale_ref, nw_ref, w_ref, o_ref, h_ref):
    @pl.when(pl.program_id(2) == 0)
    def _():
        x = x_ref[0]
        r = lax.rsqrt(jnp.mean(x * x, axis=-1, keepdims=True) + EPS)
        h = (x * r) * nw_ref[...] * (1.0 + scale_ref[0]) + shift_ref[0]
        h_ref[...] = h.astype(BF16)

    o_ref[0] = _bdot_nt(h_ref[...], w_ref[...]).astype(BF16)


def _inproj_call(x, mod3, norm_w, w_all, *, mod_row, w_col, n_col, tn, name):
    nb, t, d = x.shape
    tm = 1024
    return pl.pallas_call(
        _inproj_kernel,
        grid=(nb, t // tm, n_col),
        in_specs=[pl.BlockSpec((1, tm, d), lambda bi, i, j: (bi, i, 0)),
                  pl.BlockSpec((1, 1, d), lambda bi, i, j: (mod_row(bi), 0, 0)),
                  pl.BlockSpec((1, 1, d), lambda bi, i, j: (mod_row(bi), 0, 1)),
                  pl.BlockSpec((1, d), lambda bi, i, j: (0, 0)),
                  pl.BlockSpec((tn, d), lambda bi, i, j: (w_col(j), 0))],
        out_specs=pl.BlockSpec((1, tm, tn), lambda bi, i, j: (bi, i, j)),
        out_shape=jax.ShapeDtypeStruct((nb, t, n_col * tn), BF16),
        scratch_shapes=[pltpu.VMEM((tm, d), BF16)],
        compiler_params=_params(3),
        name=name,
    )(x, mod3, mod3, norm_w, w_all)


PREP_T = 256
HALO = 16


CONV_OFFS = (-2, -1, 1, 2)


def _prep_kernel(*refs, latent):
    if latent:
        (main_ref, prev_ref, next_ref, small_ref, shift_ref, convw_ref, alog_ref, dtb_ref, knw_ref,
         cos_ref, sin_ref, g_ref, bg_ref, kn_ref) = refs
        i = pl.program_id(1)
        has_prev = i > 0
        has_next = i < pl.num_programs(1) - 1
        row8 = lax.broadcasted_iota(jnp.int32, (8, 1024), 0)
    else:
        (main_ref, small_ref, shift_ref, convw_ref, alog_ref, dtb_ref, knw_ref,
         g_ref, bg_ref, kn_ref) = refs
    t = PREP_T
    for c in range(3):
        cs = slice(c * 1024, (c + 1) * 1024)
        w = [convw_ref[k:k + 1, cs] for k in range(A_CONV)]
        x = main_ref[0, :, cs]
        y = jnp.dot(shift_ref[...], x, preferred_element_type=F32)
        acc = x.astype(F32) * w[2]
        for n, k in enumerate((0, 1, 3, 4)):
            acc = acc + y[n * t:(n + 1) * t] * w[k]
        if latent:
            pv = jnp.where(has_prev, prev_ref[0, HALO - 8:, cs].astype(F32), 0.0)
            nx = jnp.where(has_next, next_ref[0, :8, cs].astype(F32), 0.0)
            top = (jnp.where(row8 < 2, pltpu.roll(pv, 2, 0), 0.0) * w[0]
                   + jnp.where(row8 < 1, pltpu.roll(pv, 1, 0), 0.0) * w[1])
            bot = (jnp.where(row8 >= 7, pltpu.roll(nx, 7, 0), 0.0) * w[3]
                   + jnp.where(row8 >= 6, pltpu.roll(nx, 6, 0), 0.0) * w[4])
            acc = jnp.concatenate([acc[:8] + top, acc[8:t - 8], acc[t - 8:] + bot], axis=0)
        act = _silu(acc)
        if c == 2:
            g_ref[0, :, cs] = act.astype(BF16)
        else:
            mul = HD ** -0.5 if c == 0 else 1.0
            for h in range(HEADS):
                a = act[:, h * HD:(h + 1) * HD]
                r = lax.rsqrt(jnp.sum(a * a, axis=-1, keepdims=True) + EPS)
                g_ref[0, :, c * 1024 + h * HD:c * 1024 + (h + 1) * HD] = (a * (r * mul)).astype(BF16)

    ba = small_ref[0, :, SMALL_BA:SMALL_BA + LANES].astype(F32)
    beta = jax.nn.sigmoid(ba)
    gdec = -jnp.exp(alog_ref[...]) * _softplus(ba + dtb_ref[...])
    lane = lax.broadcasted_iota(jnp.int32, ba.shape, 1)
    bg_ref[0] = jnp.where(lane < 2 * HEADS, beta, gdec)

    for h in range(KV_HEADS):
        kh = small_ref[0, :, SMALL_BK + h * HD:SMALL_BK + (h + 1) * HD].astype(F32)
        r = lax.rsqrt(jnp.mean(kh * kh, axis=-1, keepdims=True) + EPS)
        kn = kh * r * knw_ref[...]
        if latent:
            kn = kn * cos_ref[...] + pltpu.roll(kn, HD // 2, 1) * sin_ref[...]
        kn_ref[0, :, h * HD:(h + 1) * HD] = kn.astype(BF16)


def _prep_call(p, conv_w, alog_row, dtb_row, k_norm_w, cosf, sinf, *, latent):
    b, t, _ = p.shape
    nt = t // PREP_T
    const = lambda bi, i: (0, 0)
    tile = lambda bi, i: (bi, i, 0)
    if latent:
        hb = PREP_T // HALO
        last_hb = t // HALO - 1
        small_col = COL_SMALL // 1024
        in_specs = [pl.BlockSpec((1, PREP_T, 3072), tile),
                    pl.BlockSpec((1, HALO, 3072), lambda bi, i: (bi, jnp.maximum(i * hb - 1, 0), 0)),
                    pl.BlockSpec((1, HALO, 3072), lambda bi, i: (bi, jnp.minimum((i + 1) * hb, last_hb), 0)),
                    pl.BlockSpec((1, PREP_T, 1024), lambda bi, i: (bi, i, small_col))]
        args = [p, p, p, p]
    else:
        small_col = CTX_COL_SMALL // 1024
        in_specs = [pl.BlockSpec((1, PREP_T, 3072), tile),
                    pl.BlockSpec((1, PREP_T, 1024), lambda bi, i: (bi, i, small_col))]
        args = [p, p]
    shifts = jnp.concatenate([jnp.eye(PREP_T, k=off, dtype=BF16) for off in CONV_OFFS], axis=0)
    in_specs += [pl.BlockSpec(shifts.shape, const),
                 pl.BlockSpec((A_CONV, 3072), const), pl.BlockSpec((1, LANES), const),
                 pl.BlockSpec((1, LANES), const), pl.BlockSpec((1, HD), const)]
    args += [shifts, conv_w, alog_row, dtb_row, k_norm_w]
    if latent:
        in_specs += [pl.BlockSpec((PREP_T, HD), lambda bi, i: (i, 0))] * 2
        args += [cosf, sinf]
    return pl.pallas_call(
        functools.partial(_prep_kernel, latent=latent),
        grid=(b, nt),
        in_specs=in_specs,
        out_specs=[pl.BlockSpec((1, PREP_T, 3072), tile),
                   pl.BlockSpec((1, PREP_T, LANES), tile),
                   pl.BlockSpec((1, PREP_T, KV_HEADS * HD), tile)],
        out_shape=[jax.ShapeDtypeStruct((b, t, 3072), BF16),
                   jax.ShapeDtypeStruct((b, t, LANES), F32),
                   jax.ShapeDtypeStruct((b, t, KV_HEADS * HD), BF16)],
        compiler_params=_params(2),
        name="prep_latent" if latent else "prep_ctx",
    )(*args)


def _chunk_masks(direction):
    c = CHUNK
    ii = lax.broadcasted_iota(jnp.int32, (c, c), 0)
    jj = lax.broadcasted_iota(jnp.int32, (c, c), 1)
    if direction == 0:
        return ii >= jj, ii > jj, c - 1
    return ii <= jj, ii < jj, 0


def _chunk_cumsum(bg, incl):
    tri = incl.astype(BF16)
    g1 = bg.astype(BF16)
    r1 = bg - g1.astype(F32)
    g2 = r1.astype(BF16)
    g3 = (r1 - g2.astype(F32)).astype(BF16)
    return (jnp.dot(tri, g1, preferred_element_type=F32) + jnp.dot(tri, g2, preferred_element_type=F32)
            + jnp.dot(tri, g3, preferred_element_type=F32))


def _gdn_kernel(*refs, latent):
    step = pl.program_id(1)
    if latent:
        qkv_f_ref, bg_f_ref, qkv_b_ref, bg_b_ref, s0_f_ref, s0_b_ref, o_f_ref, o_b_ref, s_f_ref, s_b_ref = refs

        @pl.when(step == 0)
        def _():
            s_f_ref[...] = s0_f_ref[...]
            s_b_ref[...] = s0_b_ref[...]

        o_refs = (o_f_ref, o_b_ref)
    else:
        qkv_f_ref, bg_f_ref, qkv_b_ref, bg_b_ref, s_f_ref, s_b_ref = refs

        @pl.when(step == 0)
        def _():
            s_f_ref[...] = jnp.zeros(s_f_ref.shape, F32)
            s_b_ref[...] = jnp.zeros(s_b_ref.shape, F32)

    state = lambda n, d, h: (s_f_ref, s_b_ref)[d].at[n, h]
    c = CHUNK
    qkv_refs = (qkv_f_ref, qkv_b_ref)
    rows = range(qkv_f_ref.shape[0])
    chains = [(n, d, h) for n in rows for d in range(2) for h in range(HEADS)]
    ii = lax.broadcasted_iota(jnp.int32, (c, c), 0)
    jj = lax.broadcasted_iota(jnp.int32, (c, c), 1)
    eye = (ii == jj).astype(F32)

    masks = [_chunk_masks(d) for d in range(2)]
    bg_refs = (bg_f_ref, bg_b_ref)
    bgs = {(n, d): bg_refs[d][n] for n in rows for d in range(2)}
    gcs = {nd: _chunk_cumsum(bgs[nd], masks[nd[1]][0]) for nd in bgs}
    gcts = {nd: jnp.concatenate([gcs[nd], jnp.zeros((LANES - c, LANES), F32)], axis=0).T for nd in bgs}

    q, k, v, kf, beta, gcol, glast, decay = {}, {}, {}, {}, {}, {}, {}, {}
    for ch in chains:
        n, d, h = ch
        incl, _, last = masks[d]
        cb = d * HEADS + h
        cg = 2 * HEADS + cb
        q[ch] = qkv_refs[d][n, :, h * HD:(h + 1) * HD]
        k[ch] = qkv_refs[d][n, :, 1024 + h * HD:1024 + (h + 1) * HD]
        v[ch] = qkv_refs[d][n, :, 2048 + h * HD:2048 + (h + 1) * HD]
        kf[ch] = k[ch].astype(F32)
        beta[ch] = bgs[n, d][:, cb:cb + 1]
        gcol[ch] = gcs[n, d][:, cg:cg + 1]
        grow = gcts[n, d][cg:cg + 1, 0:c]
        glast[ch] = gcs[n, d][last:last + 1, cg:cg + 1]
        decay[ch] = jnp.where(incl, jnp.exp(jnp.where(incl, gcol[ch] - grow, 0.0)), 0.0)

    kk = {ch: _bdot_nt(k[ch], k[ch]) for ch in chains}
    l_mat = {ch: beta[ch] * kk[ch] * jnp.where(masks[ch[1]][1], decay[ch], 0.0) for ch in chains}
    if latent:
        qk = {ch: _bdot_nt(q[ch], k[ch]) for ch in chains}
        attn = {ch: qk[ch] * decay[ch] for ch in chains}

    lp = dict(l_mat)
    t_inv = {ch: eye - l_mat[ch] for ch in chains}
    for _ in range(c.bit_length() - 2):
        lp = {ch: _bdot(lp[ch], lp[ch]) for ch in chains}
        t_inv = {ch: t_inv[ch] + _bdot(t_inv[ch], lp[ch]) for ch in chains}

    eg = {ch: jnp.exp(gcol[ch]) for ch in chains}
    rhs = {ch: jnp.concatenate([v[ch].astype(F32) * beta[ch], kf[ch] * (beta[ch] * eg[ch])], axis=1)
           for ch in chains}
    uw = {ch: _bdot(t_inv[ch], rhs[ch]) for ch in chains}
    s = {ch: state(*ch)[...] for ch in chains}
    if latent:
        lhs = {ch: jnp.concatenate([uw[ch][:, HD:], q[ch].astype(F32) * eg[ch]], axis=0) for ch in chains}
    else:
        lhs = {ch: uw[ch][:, HD:] for ch in chains}
    ws_qs = {ch: _bdot(lhs[ch], s[ch]) for ch in chains}
    v_new = {ch: uw[ch][:, :HD] - ws_qs[ch][:c] for ch in chains}
    k_dec = {ch: kf[ch] * jnp.exp(glast[ch] - gcol[ch]) for ch in chains}
    kv = {ch: _bdot_tn(k_dec[ch], v_new[ch]) for ch in chains}
    if latent:
        av = {ch: _bdot(attn[ch], v_new[ch]) for ch in chains}
    for ch in chains:
        n, d, h = ch
        if latent:
            o_refs[d][n, :, h * HD:(h + 1) * HD] = (ws_qs[ch][c:] + av[ch]).astype(BF16)
        state(*ch)[...] = s[ch] * jnp.exp(glast[ch]) + kv[ch]


GDN_ROWS = 4


def _gdn_call(g, bg, states=None):
    b, t, _ = g.shape
    nc = t // CHUNK
    nr = GDN_ROWS
    latent = states is not None
    fwd = lambda bi, s: (bi, s, 0)
    bwd = lambda bi, s: (bi, nc - 1 - s, 0)
    in_specs = [pl.BlockSpec((nr, CHUNK, 3072), fwd), pl.BlockSpec((nr, CHUNK, LANES), fwd),
                pl.BlockSpec((nr, CHUNK, 3072), bwd), pl.BlockSpec((nr, CHUNK, LANES), bwd)]
    state_spec = pl.BlockSpec((nr, HEADS, HD, HD), lambda bi, s: (bi, 0, 0, 0))
    state_shape = jax.ShapeDtypeStruct((b, HEADS, HD, HD), F32)
    if latent:
        return pl.pallas_call(
            functools.partial(_gdn_kernel, latent=True),
            grid=(b // nr, nc),
            in_specs=in_specs + [state_spec, state_spec],
            out_specs=[pl.BlockSpec((nr, CHUNK, HEADS * HD), fwd),
                       pl.BlockSpec((nr, CHUNK, HEADS * HD), bwd)],
            out_shape=[jax.ShapeDtypeStruct((b, t, HEADS * HD), BF16)] * 2,
            scratch_shapes=[pltpu.VMEM((nr, HEADS, HD, HD), F32)] * 2,
            compiler_params=_params(2),
            name="gdn_latent",
        )(g, bg, g, bg, *states)
    return pl.pallas_call(
        functools.partial(_gdn_kernel, latent=False),
        grid=(b // nr, nc),
        in_specs=in_specs,
        out_specs=[state_spec, state_spec],
        out_shape=[state_shape, state_shape],
        compiler_params=_params(2),
        name="gdn_ctx",
    )(g, bg, g, bg)


ATT_TQ = 256
Q_PER_KV = HEADS // KV_HEADS
Q_SCALE = HD ** -0.5 * math.log2(math.e)


def _attn_kernel(q_ref, z_ref, kl_ref, kc_ref, vl_ref, vc_ref, qnw_ref, cos_ref, sin_ref, o_ref):
    def scores(j):
        g = slice((j // Q_PER_KV) * HD, (j // Q_PER_KV + 1) * HD)
        qh = q_ref[0, :, j * HD:(j + 1) * HD].astype(F32)
        r = lax.rsqrt(jnp.mean(qh * qh, axis=-1, keepdims=True) + EPS)
        qn = qh * r * qnw_ref[...]
        qr = (qn * cos_ref[...] + pltpu.roll(qn, HD // 2, 1) * sin_ref[...]) * Q_SCALE
        return _bdot_nt(qr, kl_ref[0, :, g]), _bdot_nt(qr, kc_ref[0, :, g])

    nxt = scores(0)
    for j in range(HEADS):
        sl = slice(j * HD, (j + 1) * HD)
        g = slice((j // Q_PER_KV) * HD, (j // Q_PER_KV + 1) * HD)
        s_l, s_c = nxt
        if j + 1 < HEADS:
            nxt = scores(j + 1)
        m = jnp.maximum(jnp.max(s_l, axis=-1, keepdims=True), jnp.max(s_c, axis=-1, keepdims=True))
        e_l = jnp.exp2(s_l - m)
        e_c = jnp.exp2(s_c - m)
        den = jnp.sum(e_l, axis=-1, keepdims=True) + jnp.sum(e_c, axis=-1, keepdims=True)
        o = (jnp.dot(e_l.astype(BF16), vl_ref[0, :, g], preferred_element_type=F32)
             + jnp.dot(e_c.astype(BF16), vc_ref[0, :, g], preferred_element_type=F32)) / den
        o_ref[0, :, sl] = (o * _silu(z_ref[0, :, sl].astype(F32))).astype(BF16)


def _attn_call(p, p_ctx, kn, kn_ctx, q_norm_w, cosf, sinf):
    b = p.shape[0]
    qw = HEADS * HD
    kvw = KV_HEADS * HD
    lat_v = (COL_SMALL + SMALL_BV) // kvw
    ctx_v = (CTX_COL_SMALL + SMALL_BV) // kvw
    return pl.pallas_call(
        _attn_kernel,
        grid=(b, SEQ // ATT_TQ),
        in_specs=[pl.BlockSpec((1, ATT_TQ, qw), lambda bi, i: (bi, i, COL_BQ // qw)),
                  pl.BlockSpec((1, ATT_TQ, qw), lambda bi, i: (bi, i, COL_BZ // qw)),
                  pl.BlockSpec((1, SEQ, kvw), lambda bi, i: (bi, 0, 0)),
                  pl.BlockSpec((1, CTX_LEN, kvw), lambda bi, i: (bi, 0, 0)),
                  pl.BlockSpec((1, SEQ, kvw), lambda bi, i: (bi, 0, lat_v)),
                  pl.BlockSpec((1, CTX_LEN, kvw), lambda bi, i: (bi, 0, ctx_v)),
                  pl.BlockSpec((1, HD), lambda bi, i: (0, 0)),
                  pl.BlockSpec((ATT_TQ, HD), lambda bi, i: (i, 0)),
                  pl.BlockSpec((ATT_TQ, HD), lambda bi, i: (i, 0))],
        out_specs=pl.BlockSpec((1, ATT_TQ, qw), lambda bi, i: (bi, i, 0)),
        out_shape=jax.ShapeDtypeStruct((b, SEQ, qw), BF16),
        compiler_params=_params(2),
        name="attn",
    )(p, p, kn, kn_ctx, p, p_ctx, q_norm_w, cosf, sinf)


MERGE_T = 512


def _merge_kernel(of_ref, ob_ref, az_ref, bl_ref, bra_ref, brb_ref, x_ref, gate_ref, anw_ref,
                  wpa_ref, wpb_ref, wo_ref, out_ref, a_ref):
    o = of_ref[0].astype(F32) + ob_ref[0].astype(F32)
    for h in range(HEADS):
        sl = slice(h * HD, (h + 1) * HD)
        oh = o[:, sl]
        r = lax.rsqrt(jnp.mean(oh * oh, axis=-1, keepdims=True) + EPS)
        a_ref[:, sl] = (oh * r * anw_ref[...] * _silu(az_ref[0, :, sl].astype(F32))).astype(BF16)
    pa = jnp.dot(a_ref[...], wpa_ref[...], preferred_element_type=F32)
    pb = jnp.dot(bl_ref[0], wpb_ref[...], preferred_element_type=F32)
    m = (jax.nn.sigmoid(bra_ref[0].astype(F32)) * pa + jax.nn.sigmoid(brb_ref[0].astype(F32)) * pb)
    y = jnp.dot(m.astype(BF16), wo_ref[...], preferred_element_type=F32)
    out_ref[0] = x_ref[0] + gate_ref[0] * y


def _merge_call(o_f, o_b, p, b_lat, x, mod3, a_norm_w, wpa, wpb, wo):
    b, t, d = x.shape
    tm = MERGE_T
    tile = lambda bi, i: (bi, i, 0)
    const = lambda bi, i: (0, 0)
    return pl.pallas_call(
        _merge_kernel,
        grid=(b, t // tm),
        in_specs=[pl.BlockSpec((1, tm, 1024), tile),
                  pl.BlockSpec((1, tm, 1024), tile),
                  pl.BlockSpec((1, tm, 1024), lambda bi, i: (bi, i, COL_AZ // 1024)),
                  pl.BlockSpec((1, tm, 1024), tile),
                  pl.BlockSpec((1, tm, d), lambda bi, i: (bi, i, COL_BRA // d)),
                  pl.BlockSpec((1, tm, d), lambda bi, i: (bi, i, COL_BRB // d)),
                  pl.BlockSpec((1, tm, d), tile),
                  pl.BlockSpec((1, 1, d), lambda bi, i: (bi, 0, 2)),
                  pl.BlockSpec((1, HD), const),
                  pl.BlockSpec((1024, d), const, pipeline_mode=pl.Buffered(1)),
                  pl.BlockSpec((1024, d), const, pipeline_mode=pl.Buffered(1)),
                  pl.BlockSpec((d, d), const, pipeline_mode=pl.Buffered(1))],
        out_specs=pl.BlockSpec((1, tm, d), tile),
        out_shape=jax.ShapeDtypeStruct((b, t, d), F32),
        scratch_shapes=[pltpu.VMEM((tm, 1024), BF16)],
        compiler_params=_params(2),
        name="merge",
    )(o_f, o_b, p, b_lat, p, p, x, mod3, a_norm_w, wpa, wpb, wo)


def _rope_tables():
    t = jnp.arange(SEQ)
    row = (t // GRID_W).astype(F32)
    col = (t % GRID_W).astype(F32)
    n_freq = HD // 4
    inv = ROPE_THETA ** (-jnp.arange(n_freq, dtype=F32) / n_freq)
    ang = jnp.concatenate([row[:, None] * inv, col[:, None] * inv], axis=-1)
    cos, sin = jnp.cos(ang), jnp.sin(ang)
    return jnp.concatenate([cos, cos], axis=-1), jnp.concatenate([-sin, sin], axis=-1)


def _gate_row(a):
    return jnp.zeros((1, LANES), F32).at[0, 2 * HEADS:4 * HEADS].set(a.reshape(-1).astype(F32))


def _layer(x, ctx, c, c_ctx, w_mod, b_mod, norm_w, w_in_t, conv_w, a_log, dt_bias,
           a_norm_w, q_norm_w, k_norm_w, w_proj_a, w_proj_b, w_out):
    b = x.shape[0]
    d = D_MODEL
    w_all = _wprep_call(w_in_t)
    wpa = w_proj_a.astype(BF16)
    wpb = w_proj_b.astype(BF16)
    wo = w_out.astype(BF16)

    cc = jnp.concatenate([c, c_ctx[None, :], jnp.zeros((8 - b - 1, d), F32)], axis=0)
    mod3 = _mod_call(cc, w_mod, b_mod[None, :]).reshape(8, 1, 3 * d)

    nw = norm_w[None, :]
    p = _inproj_call(x, mod3, nw, w_all, mod_row=lambda bi: bi, w_col=lambda j: j,
                     n_col=P_COLS // INPROJ_TN, tn=INPROJ_TN, name="inproj_latent")
    small_blk = COL_SMALL // 1024
    p_ctx = _inproj_call(ctx.reshape(1, b * CTX_LEN, d), mod3, nw, w_all, mod_row=lambda bi: b,
                         w_col=lambda j: jnp.where(j < 3, j, small_blk), n_col=CTX_COLS // 1024,
                         tn=1024, name="inproj_ctx").reshape(b, CTX_LEN, CTX_COLS)

    cosf, sinf = _rope_tables()
    alog_row, dtb_row, knw = _gate_row(a_log), _gate_row(dt_bias), k_norm_w[None, :]
    g, bg, kn = _prep_call(p, conv_w, alog_row, dtb_row, knw, cosf, sinf, latent=True)
    g_ctx, bg_ctx, kn_ctx = _prep_call(p_ctx, conv_w, alog_row, dtb_row, knw, None, None, latent=False)
    states = _gdn_call(g_ctx, bg_ctx)
    o_f, o_b = _gdn_call(g, bg, states)
    b_lat = _attn_call(p, p_ctx, kn, kn_ctx, q_norm_w[None, :], cosf, sinf)
    return _merge_call(o_f, o_b, p, b_lat, x, mod3, a_norm_w[None, :], wpa, wpb, wo)


def kernel(x, c, ctx, c_ctx, w_mod, b_mod, norm_w, w_in, conv_w, a_log, dt_bias, a_norm_w,
           q_norm_w, k_norm_w, w_proj_a, w_proj_b, w_out):
    assert w_mod.shape[0] == 1
    w_in_t = jnp.swapaxes(w_in, 1, 2).reshape(w_in.shape[2], w_in.shape[1])
    return _layer(x, ctx, c, c_ctx, w_mod[0], b_mod[0], norm_w[0], w_in_t, conv_w[0], a_log[0],
                  dt_bias[0], a_norm_w[0], q_norm_w[0], k_norm_w[0], w_proj_a[0], w_proj_b[0],
                  w_out[0])
```

```python
import functools
import math

import jax
import jax.numpy as jnp
from jax import lax
from jax.experimental import pallas as pl
from jax.experimental.pallas import tpu as pltpu

F32 = jnp.float32
BF16 = jnp.bfloat16

D_MODEL = 2048
SEQ = 2048
CTX_LEN = 256
GRID_W = 64
EPS = 1e-6
HEADS = 8
HD = 128
KV_HEADS = 2
A_CONV = 5
ROPE_THETA = 10000.0
CHUNK = 64

COL_AQ, COL_AK, COL_AV, COL_AZ, COL_BQ, COL_BZ, COL_BRA, COL_BRB, COL_SMALL = (
    0, 1024, 2048, 3072, 4096, 5120, 6144, 8192, 10240)
P_COLS = 11264
SMALL_BK, SMALL_BV, SMALL_BA = 0, 256, 512
CTX_COLS = 4096
CTX_COL_SMALL = 3072

VMEM_LIMIT = 56 * 1024 * 1024
LANES = 128
INPROJ_TN = 1024


def _silu(x):
    return x * jax.nn.sigmoid(x)


def _softplus(x):
    return jnp.maximum(x, 0.0) + jnp.log1p(jnp.exp(-jnp.abs(x)))


def _bdot(a, b):
    return jnp.dot(a.astype(BF16), b.astype(BF16), preferred_element_type=F32)


def _bdot_nt(a, b):
    return lax.dot_general(a.astype(BF16), b.astype(BF16), (((1,), (1,)), ((), ())),
                           preferred_element_type=F32)


def _bdot_tn(a, b):
    return lax.dot_general(a.astype(BF16), b.astype(BF16), (((0,), (0,)), ((), ())),
                           preferred_element_type=F32)


def _params(n_axes):
    return pltpu.CompilerParams(dimension_semantics=("arbitrary",) * n_axes,
                                vmem_limit_bytes=VMEM_LIMIT)


WP_ROWS = 512
WP_SHIFT = 32
WP_NBLK = P_COLS // WP_ROWS


def _wprep_src(n):
    return jnp.where(n < 10, n, jnp.where(n < 20, n + 1, jnp.where(n == 20, 10, 8)))


def _wprep_kernel(main_ref, extra_ref, o_ref):
    n = pl.program_id(0)
    keep = WP_ROWS - WP_SHIFT

    @pl.when(n < 8)
    def _():
        o_ref[...] = main_ref[...].astype(BF16)

    @pl.when(jnp.logical_and(n >= 8, n < WP_NBLK - 1))
    def _():
        o_ref[:keep] = main_ref[WP_SHIFT:].astype(BF16)
        o_ref[keep:] = extra_ref[...].astype(BF16)

    @pl.when(n == WP_NBLK - 1)
    def _():
        o_ref[:WP_SHIFT] = main_ref[:WP_SHIFT].astype(BF16)
        o_ref[WP_SHIFT:] = jnp.zeros((keep, o_ref.shape[1]), BF16)


def _wprep_call(w_t):
    d = w_t.shape[1]
    ratio = WP_ROWS // WP_SHIFT
    return pl.pallas_call(
        _wprep_kernel,
        grid=(WP_NBLK,),
        in_specs=[pl.BlockSpec((WP_ROWS, d), lambda n: (_wprep_src(n), 0)),
                  pl.BlockSpec((WP_SHIFT, d), lambda n: (ratio * _wprep_src(n) + ratio, 0))],
        out_specs=pl.BlockSpec((WP_ROWS, d), lambda n: (n, 0)),
        out_shape=jax.ShapeDtypeStruct((P_COLS, d), BF16),
        compiler_params=_params(1),
        name="wprep",
    )(w_t, w_t)


def _mod_kernel(c_ref, w_ref, b_ref, o_ref):
    o_ref[...] = _bdot(_silu(c_ref[...]), w_ref[...]) + b_ref[...]


def _mod_call(cc, w_mod, b_mod):
    tn = 768
    n = w_mod.shape[1]
    return pl.pallas_call(
        _mod_kernel,
        grid=(n // tn,),
        in_specs=[pl.BlockSpec((8, D_MODEL), lambda j: (0, 0)),
                  pl.BlockSpec((D_MODEL, tn), lambda j: (0, j)),
                  pl.BlockSpec((1, tn), lambda j: (0, j))],
        out_specs=pl.BlockSpec((8, tn), lambda j: (0, j)),
        out_shape=jax.ShapeDtypeStruct((8, n), F32),
        compiler_params=_params(1),
        name="mod",
    )(cc, w_mod, b_mod)


def _inproj_kernel(x_ref, shift_ref, scale_ref, nw_ref, w_ref, o_ref, h_ref):
    @pl.when(pl.program_id(2) == 0)
    def _():
        x = x_ref[0]
        r = lax.rsqrt(jnp.mean(x * x, axis=-1, keepdims=True) + EPS)
        gain = nw_ref[...] * (1.0 + scale_ref[0])
        h_ref[...] = ((x * r) * gain + shift_ref[0]).astype(BF16)

    o_ref[0] = _bdot_nt(h_ref[...], w_ref[...]).astype(BF16)


def _inproj_call(x, mod3, norm_w, w_all, *, mod_row, w_col, n_col, tn, name):
    nb, t, d = x.shape
    tm = 1024
    return pl.pallas_call(
        _inproj_kernel,
        grid=(nb, t // tm, n_col),
        in_specs=[pl.BlockSpec((1, tm, d), lambda bi, i, j: (bi, i, 0)),
                  pl.BlockSpec((1, 1, d), lambda bi, i, j: (mod_row(bi), 0, 0)),
                  pl.BlockSpec((1, 1, d), lambda bi, i, j: (mod_row(bi), 0, 1)),
                  pl.BlockSpec((1, d), lambda bi, i, j: (0, 0)),
                  pl.BlockSpec((tn, d), lambda bi, i, j: (w_col(j), 0))],
        out_specs=pl.BlockSpec((1, tm, tn), lambda bi, i, j: (bi, i, j)),
        out_shape=jax.ShapeDtypeStruct((nb, t, n_col * tn), BF16),
        scratch_shapes=[pltpu.VMEM((tm, d), BF16)],
        compiler_params=_params(3),
        name=name,
    )(x, mod3, mod3, norm_w, w_all)


PREP_T = 256
HALO = 16


CONV_OFFS = (-2, -1, 1, 2)


def _prep_kernel(*refs, latent):
    if latent:
        (main_ref, prev_ref, next_ref, small_ref, shift_ref, convw_ref, alog_ref, dtb_ref, knw_ref,
         cos_ref, sin_ref, g_ref, bg_ref, kn_ref) = refs
        i = pl.program_id(1)
        has_prev = i > 0
        has_next = i < pl.num_programs(1) - 1
        row8 = lax.broadcasted_iota(jnp.int32, (8, 1024), 0)
    else:
        (main_ref, small_ref, shift_ref, convw_ref, alog_ref, dtb_ref, knw_ref,
         g_ref, bg_ref, kn_ref) = refs
    t = PREP_T
    for c in range(3):
        cs = slice(c * 1024, (c + 1) * 1024)
        w = [convw_ref[k:k + 1, cs] for k in range(A_CONV)]
        x = main_ref[0, :, cs]
        y = jnp.dot(shift_ref[...], x, preferred_element_type=F32)
        acc = x.astype(F32) * w[2]
        for n, k in enumerate((0, 1, 3, 4)):
            acc = acc + y[n * t:(n + 1) * t] * w[k]
        if latent:
            pv = jnp.where(has_prev, prev_ref[0, HALO - 8:, cs].astype(F32), 0.0)
            nx = jnp.where(has_next, next_ref[0, :8, cs].astype(F32), 0.0)
            top = (jnp.where(row8 < 2, pltpu.roll(pv, 2, 0), 0.0) * w[0]
                   + jnp.where(row8 < 1, pltpu.roll(pv, 1, 0), 0.0) * w[1])
            bot = (jnp.where(row8 >= 7, pltpu.roll(nx, 7, 0), 0.0) * w[3]
                   + jnp.where(row8 >= 6, pltpu.roll(nx, 6, 0), 0.0) * w[4])
            acc = jnp.concatenate([acc[:8] + top, acc[8:t - 8], acc[t - 8:] + bot], axis=0)
        act = _silu(acc)
        if c == 2:
            g_ref[0, :, cs] = act.astype(BF16)
        else:
            mul = HD ** -0.5 if c == 0 else 1.0
            for h in range(HEADS):
                a = act[:, h * HD:(h + 1) * HD]
                r = lax.rsqrt(jnp.sum(a * a, axis=-1, keepdims=True) + EPS)
                g_ref[0, :, c * 1024 + h * HD:c * 1024 + (h + 1) * HD] = (a * (r * mul)).astype(BF16)

    ba = small_ref[0, :, SMALL_BA:SMALL_BA + LANES].astype(F32)
    beta = jax.nn.sigmoid(ba)
    gdec = -jnp.exp(alog_ref[...]) * _softplus(ba + dtb_ref[...])
    lane = lax.broadcasted_iota(jnp.int32, ba.shape, 1)
    bg_ref[0] = jnp.where(lane < 2 * HEADS, beta, gdec)

    for h in range(KV_HEADS):
        kh = small_ref[0, :, SMALL_BK + h * HD:SMALL_BK + (h + 1) * HD].astype(F32)
        r = lax.rsqrt(jnp.mean(kh * kh, axis=-1, keepdims=True) + EPS)
        kn = kh * r * knw_ref[...]
        if latent:
            kn = kn * cos_ref[...] + pltpu.roll(kn, HD // 2, 1) * sin_ref[...]
        kn_ref[0, :, h * HD:(h + 1) * HD] = kn.astype(BF16)


def _prep_call(p, conv_w, alog_row, dtb_row, k_norm_w, cosf, sinf, *, latent):
    b, t, _ = p.shape
    nt = t // PREP_T
    const = lambda bi, i: (0, 0)
    tile = lambda bi, i: (bi, i, 0)
    if latent:
        hb = PREP_T // HALO
        last_hb = t // HALO - 1
        small_col = COL_SMALL // 1024
        in_specs = [pl.BlockSpec((1, PREP_T, 3072), tile),
                    pl.BlockSpec((1, HALO, 3072), lambda bi, i: (bi, jnp.maximum(i * hb - 1, 0), 0)),
                    pl.BlockSpec((1, HALO, 3072), lambda bi, i: (bi, jnp.minimum((i + 1) * hb, last_hb), 0)),
                    pl.BlockSpec((1, PREP_T, 1024), lambda bi, i: (bi, i, small_col))]
        args = [p, p, p, p]
    else:
        small_col = CTX_COL_SMALL // 1024
        in_specs = [pl.BlockSpec((1, PREP_T, 3072), tile),
                    pl.BlockSpec((1, PREP_T, 1024), lambda bi, i: (bi, i, small_col))]
        args = [p, p]
    shifts = jnp.concatenate([jnp.eye(PREP_T, k=off, dtype=BF16) for off in CONV_OFFS], axis=0)
    in_specs += [pl.BlockSpec(shifts.shape, const),
                 pl.BlockSpec((A_CONV, 3072), const), pl.BlockSpec((1, LANES), const),
                 pl.BlockSpec((1, LANES), const), pl.BlockSpec((1, HD), const)]
    args += [shifts, conv_w, alog_row, dtb_row, k_norm_w]
    if latent:
        in_specs += [pl.BlockSpec((PREP_T, HD), lambda bi, i: (i, 0))] * 2
        args += [cosf, sinf]
    return pl.pallas_call(
        functools.partial(_prep_kernel, latent=latent),
        grid=(b, nt),
        in_specs=in_specs,
        out_specs=[pl.BlockSpec((1, PREP_T, 3072), tile),
                   pl.BlockSpec((1, PREP_T, LANES), tile),
                   pl.BlockSpec((1, PREP_T, KV_HEADS * HD), tile)],
        out_shape=[jax.ShapeDtypeStruct((b, t, 3072), BF16),
                   jax.ShapeDtypeStruct((b, t, LANES), F32),
                   jax.ShapeDtypeStruct((b, t, KV_HEADS * HD), BF16)],
        compiler_params=_params(2),
        name="prep_latent" if latent else "prep_ctx",
    )(*args)


def _chunk_masks(direction):
    c = CHUNK
    ii = lax.broadcasted_iota(jnp.int32, (c, c), 0)
    jj = lax.broadcasted_iota(jnp.int32, (c, c), 1)
    if direction == 0:
        return ii >= jj, ii > jj, c - 1
    return ii <= jj, ii < jj, 0


def _chunk_cumsum(bg, incl):
    tri = incl.astype(BF16)
    g1 = bg.astype(BF16)
    r1 = bg - g1.astype(F32)
    g2 = r1.astype(BF16)
    g3 = (r1 - g2.astype(F32)).astype(BF16)
    return (jnp.dot(tri, g1, preferred_element_type=F32) + jnp.dot(tri, g2, preferred_element_type=F32)
            + jnp.dot(tri, g3, preferred_element_type=F32))


def _gdn_kernel(*refs, latent):
    step = pl.program_id(1)
    if latent:
        qkv_f_ref, bg_f_ref, qkv_b_ref, bg_b_ref, s0_f_ref, s0_b_ref, o_f_ref, o_b_ref, s_f_ref, s_b_ref = refs

        @pl.when(step == 0)
        def _():
            s_f_ref[...] = s0_f_ref[...]
            s_b_ref[...] = s0_b_ref[...]

        o_refs = (o_f_ref, o_b_ref)
    else:
        qkv_f_ref, bg_f_ref, qkv_b_ref, bg_b_ref, s_f_ref, s_b_ref = refs

        @pl.when(step == 0)
        def _():
            s_f_ref[...] = jnp.zeros(s_f_ref.shape, F32)
            s_b_ref[...] = jnp.zeros(s_b_ref.shape, F32)

    state = lambda n, d, h: (s_f_ref, s_b_ref)[d].at[n, h]
    c = CHUNK
    qkv_refs = (qkv_f_ref, qkv_b_ref)
    rows = range(qkv_f_ref.shape[0])
    chains = [(n, d, h) for n in rows for d in range(2) for h in range(HEADS)]

    masks = [_chunk_masks(d) for d in range(2)]
    bg_refs = (bg_f_ref, bg_b_ref)
    bgs = {(n, d): bg_refs[d][n] for n in rows for d in range(2)}
    gcs = {nd: _chunk_cumsum(bgs[nd], masks[nd[1]][0]) for nd in bgs}
    gcts = {nd: jnp.concatenate([gcs[nd], jnp.zeros((LANES - c, LANES), F32)], axis=0).T for nd in bgs}

    q, k, v, kf, beta, gcol, glast, decay = {}, {}, {}, {}, {}, {}, {}, {}
    for ch in chains:
        n, d, h = ch
        incl, _, last = masks[d]
        cb = d * HEADS + h
        cg = 2 * HEADS + cb
        q[ch] = qkv_refs[d][n, :, h * HD:(h + 1) * HD]
        k[ch] = qkv_refs[d][n, :, 1024 + h * HD:1024 + (h + 1) * HD]
        v[ch] = qkv_refs[d][n, :, 2048 + h * HD:2048 + (h + 1) * HD]
        kf[ch] = k[ch].astype(F32)
        beta[ch] = bgs[n, d][:, cb:cb + 1]
        gcol[ch] = gcs[n, d][:, cg:cg + 1]
        grow = gcts[n, d][cg:cg + 1, 0:c]
        glast[ch] = gcs[n, d][last:last + 1, cg:cg + 1]
        decay[ch] = jnp.where(incl, jnp.exp(jnp.where(incl, gcol[ch] - grow, 0.0)), 0.0)

    if latent:
        qk = {ch: _bdot_nt(q[ch], k[ch]) for ch in chains}
        attn = {ch: qk[ch] * decay[ch] for ch in chains}

    gp = GDN_PACK
    groups = [(n, d, hg) for n in rows for d in range(2) for hg in range(HEADS // gp)]
    members = {gr: [(gr[0], gr[1], gr[2] * gp + m) for m in range(gp)] for gr in groups}
    lane_c = lax.broadcasted_iota(jnp.int32, (c, gp * c), 1)
    row_c = lax.broadcasted_iota(jnp.int32, (c, gp * c), 0)
    eye4 = (lane_c % c == row_c).astype(F32)
    blk_c = [(lane_c // c == m).astype(BF16) for m in range(gp)]
    lane_k = lax.broadcasted_iota(jnp.int32, (c, gp * HD), 1)
    blk_k = [(lane_k // HD == m).astype(BF16) for m in range(gp)]

    def bd(x4):
        xb = x4.astype(BF16)
        return jnp.concatenate([xb * blk_c[m] for m in range(gp)], axis=0)

    def pdot(a, b):
        return jnp.dot(a.astype(BF16), b, preferred_element_type=F32)

    k4 = {(n, d, hg): qkv_refs[d][n, :, 1024 + hg * gp * HD:1024 + (hg + 1) * gp * HD]
          for n, d, hg in groups}
    kk4 = {gr: _bdot_nt(k4[gr], jnp.concatenate([k4[gr] * blk_k[m] for m in range(gp)], axis=0))
           for gr in groups}
    l4 = {gr: kk4[gr] * jnp.concatenate(
        [beta[ch] * jnp.where(masks[ch[1]][1], decay[ch], 0.0) for ch in members[gr]], axis=1)
        for gr in groups}

    t4 = {gr: eye4 - l4[gr] for gr in groups}
    lp4 = {gr: pdot(l4[gr], bd(l4[gr])) for gr in groups}
    n_sq = c.bit_length() - 2
    for lvl in range(n_sq):
        last_lvl = lvl == n_sq - 1
        lhs4 = {gr: t4[gr] if last_lvl else jnp.concatenate([t4[gr], lp4[gr]], axis=0) for gr in groups}
        res = {gr: pdot(lhs4[gr], bd(lp4[gr])) for gr in groups}
        t4 = {gr: t4[gr] + res[gr][:c] for gr in groups}
        if not last_lvl:
            lp4 = {gr: res[gr][c:] for gr in groups}

    t0 = {gr: t4[gr].astype(BF16) for gr in groups}
    a4 = {gr: eye4 + l4[gr] for gr in groups}
    a_hi = {gr: a4[gr].astype(BF16) for gr in groups}
    a_lo = {gr: (a4[gr] - a_hi[gr].astype(F32)).astype(BF16) for gr in groups}
    bd_t0 = {gr: bd(t0[gr]) for gr in groups}
    at = {gr: pdot(jnp.concatenate([a_hi[gr], a_lo[gr]], axis=0), bd_t0[gr]) for gr in groups}
    resid = {gr: eye4 - (at[gr][:c] + at[gr][c:]) for gr in groups}
    t1 = {gr: t0[gr].astype(F32) + pdot(t0[gr], bd(resid[gr])) for gr in groups}

    eg = {ch: jnp.exp(gcol[ch]) for ch in chains}
    rhs = {ch: jnp.concatenate([v[ch].astype(F32) * beta[ch], kf[ch] * (beta[ch] * eg[ch])], axis=1)
           for ch in chains}
    rhs4 = {gr: jnp.concatenate([rhs[ch] for ch in members[gr]], axis=0).astype(BF16) for gr in groups}
    uw4 = {gr: jnp.dot(bd(t1[gr]), rhs4[gr], preferred_element_type=F32) for gr in groups}
    uw = {ch: uw4[gr][m * c:(m + 1) * c] for gr in groups for m, ch in enumerate(members[gr])}
    s = {ch: state(*ch)[...] for ch in chains}
    if latent:
        lhs = {ch: jnp.concatenate([uw[ch][:, HD:], q[ch].astype(F32) * eg[ch]], axis=0) for ch in chains}
    else:
        lhs = {ch: uw[ch][:, HD:] for ch in chains}
    ws_qs = {ch: _bdot(lhs[ch], s[ch]) for ch in chains}
    v_new = {ch: uw[ch][:, :HD] - ws_qs[ch][:c] for ch in chains}
    k_dec = {ch: kf[ch] * jnp.exp(glast[ch] - gcol[ch]) for ch in chains}
    kv = {ch: _bdot_tn(k_dec[ch], v_new[ch]) for ch in chains}
    if latent:
        av = {ch: _bdot(attn[ch], v_new[ch]) for ch in chains}
    for ch in chains:
        n, d, h = ch
        if latent:
            o_refs[d][n, :, h * HD:(h + 1) * HD] = (ws_qs[ch][c:] + av[ch]).astype(BF16)
        state(*ch)[...] = s[ch] * jnp.exp(glast[ch]) + kv[ch]


GDN_ROWS = 4
GDN_PACK = 4


def _gdn_call(g, bg, states=None):
    b, t, _ = g.shape
    nc = t // CHUNK
    nr = GDN_ROWS
    latent = states is not None
    fwd = lambda bi, s: (bi, s, 0)
    bwd = lambda bi, s: (bi, nc - 1 - s, 0)
    in_specs = [pl.BlockSpec((nr, CHUNK, 3072), fwd), pl.BlockSpec((nr, CHUNK, LANES), fwd),
                pl.BlockSpec((nr, CHUNK, 3072), bwd), pl.BlockSpec((nr, CHUNK, LANES), bwd)]
    state_spec = pl.BlockSpec((nr, HEADS, HD, HD), lambda bi, s: (bi, 0, 0, 0))
    state_shape = jax.ShapeDtypeStruct((b, HEADS, HD, HD), F32)
    if latent:
        return pl.pallas_call(
            functools.partial(_gdn_kernel, latent=True),
            grid=(b // nr, nc),
            in_specs=in_specs + [state_spec, state_spec],
            out_specs=[pl.BlockSpec((nr, CHUNK, HEADS * HD), fwd),
                       pl.BlockSpec((nr, CHUNK, HEADS * HD), bwd)],
            out_shape=[jax.ShapeDtypeStruct((b, t, HEADS * HD), BF16)] * 2,
            scratch_shapes=[pltpu.VMEM((nr, HEADS, HD, HD), F32)] * 2,
            compiler_params=_params(2),
            name="gdn_latent",
        )(g, bg, g, bg, *states)
    return pl.pallas_call(
        functools.partial(_gdn_kernel, latent=False),
        grid=(b // nr, nc),
        in_specs=in_specs,
        out_specs=[state_spec, state_spec],
        out_shape=[state_shape, state_shape],
        compiler_params=_params(2),
        name="gdn_ctx",
    )(g, bg, g, bg)


ATT_TQ = 256
Q_PER_KV = HEADS // KV_HEADS
Q_SCALE = HD ** -0.5 * math.log2(math.e)


def _attn_kernel(q_ref, z_ref, kl_ref, kc_ref, vl_ref, vc_ref, qnw_ref, cos_ref, sin_ref, o_ref):
    def scores(j):
        g = slice((j // Q_PER_KV) * HD, (j // Q_PER_KV + 1) * HD)
        qh = q_ref[0, :, j * HD:(j + 1) * HD].astype(F32)
        r = lax.rsqrt(jnp.mean(qh * qh, axis=-1, keepdims=True) + EPS)
        qn = qh * r * qnw_ref[...]
        qr = (qn * cos_ref[...] + pltpu.roll(qn, HD // 2, 1) * sin_ref[...]) * Q_SCALE
        return _bdot_nt(qr, kl_ref[0, :, g]), _bdot_nt(qr, kc_ref[0, :, g])

    nxt = scores(0)
    for j in range(HEADS):
        sl = slice(j * HD, (j + 1) * HD)
        g = slice((j // Q_PER_KV) * HD, (j // Q_PER_KV + 1) * HD)
        s_l, s_c = nxt
        if j + 1 < HEADS:
            nxt = scores(j + 1)
        m = jnp.maximum(jnp.max(s_l, axis=-1, keepdims=True), jnp.max(s_c, axis=-1, keepdims=True))
        e_l = jnp.exp2(s_l - m)
        e_c = jnp.exp2(s_c - m)
        den = jnp.sum(e_l, axis=-1, keepdims=True) + jnp.sum(e_c, axis=-1, keepdims=True)
        o = (jnp.dot(e_l.astype(BF16), vl_ref[0, :, g], preferred_element_type=F32)
             + jnp.dot(e_c.astype(BF16), vc_ref[0, :, g], preferred_element_type=F32)) / den
        o_ref[0, :, sl] = (o * _silu(z_ref[0, :, sl].astype(F32))).astype(BF16)


def _attn_call(p, p_ctx, kn, kn_ctx, q_norm_w, cosf, sinf):
    b = p.shape[0]
    qw = HEADS * HD
    kvw = KV_HEADS * HD
    lat_v = (COL_SMALL + SMALL_BV) // kvw
    ctx_v = (CTX_COL_SMALL + SMALL_BV) // kvw
    return pl.pallas_call(
        _attn_kernel,
        grid=(b, SEQ // ATT_TQ),
        in_specs=[pl.BlockSpec((1, ATT_TQ, qw), lambda bi, i: (bi, i, COL_BQ // qw)),
                  pl.BlockSpec((1, ATT_TQ, qw), lambda bi, i: (bi, i, COL_BZ // qw)),
                  pl.BlockSpec((1, SEQ, kvw), lambda bi, i: (bi, 0, 0)),
                  pl.BlockSpec((1, CTX_LEN, kvw), lambda bi, i: (bi, 0, 0)),
                  pl.BlockSpec((1, SEQ, kvw), lambda bi, i: (bi, 0, lat_v)),
                  pl.BlockSpec((1, CTX_LEN, kvw), lambda bi, i: (bi, 0, ctx_v)),
                  pl.BlockSpec((1, HD), lambda bi, i: (0, 0)),
                  pl.BlockSpec((ATT_TQ, HD), lambda bi, i: (i, 0)),
                  pl.BlockSpec((ATT_TQ, HD), lambda bi, i: (i, 0))],
        out_specs=pl.BlockSpec((1, ATT_TQ, qw), lambda bi, i: (bi, i, 0)),
        out_shape=jax.ShapeDtypeStruct((b, SEQ, qw), BF16),
        compiler_params=_params(2),
        name="attn",
    )(p, p, kn, kn_ctx, p, p_ctx, q_norm_w, cosf, sinf)


MERGE_T = 512


def _merge_kernel(of_ref, ob_ref, az_ref, bl_ref, bra_ref, brb_ref, x_ref, gate_ref, anw_ref,
                  wpa_ref, wpb_ref, wo_ref, out_ref, a_ref):
    o = of_ref[0].astype(F32) + ob_ref[0].astype(F32)
    for h in range(HEADS):
        sl = slice(h * HD, (h + 1) * HD)
        oh = o[:, sl]
        r = lax.rsqrt(jnp.mean(oh * oh, axis=-1, keepdims=True) + EPS)
        a_ref[:, sl] = (oh * r * anw_ref[...] * _silu(az_ref[0, :, sl].astype(F32))).astype(BF16)
    pa = jnp.dot(a_ref[...], wpa_ref[...], preferred_element_type=F32)
    pb = jnp.dot(bl_ref[0], wpb_ref[...], preferred_element_type=F32)
    m = (jax.nn.sigmoid(bra_ref[0].astype(F32)) * pa + jax.nn.sigmoid(brb_ref[0].astype(F32)) * pb)
    y = jnp.dot(m.astype(BF16), wo_ref[...], preferred_element_type=F32)
    out_ref[0] = x_ref[0] + gate_ref[0] * y


def _merge_call(o_f, o_b, p, b_lat, x, mod3, a_norm_w, wpa, wpb, wo):
    b, t, d = x.shape
    tm = MERGE_T
    tile = lambda bi, i: (bi, i, 0)
    const = lambda bi, i: (0, 0)
    return pl.pallas_call(
        _merge_kernel,
        grid=(b, t // tm),
        in_specs=[pl.BlockSpec((1, tm, 1024), tile),
                  pl.BlockSpec((1, tm, 1024), tile),
                  pl.BlockSpec((1, tm, 1024), lambda bi, i: (bi, i, COL_AZ // 1024)),
                  pl.BlockSpec((1, tm, 1024), tile),
                  pl.BlockSpec((1, tm, d), lambda bi, i: (bi, i, COL_BRA // d)),
                  pl.BlockSpec((1, tm, d), lambda bi, i: (bi, i, COL_BRB // d)),
                  pl.BlockSpec((1, tm, d), tile),
                  pl.BlockSpec((1, 1, d), lambda bi, i: (bi, 0, 2)),
                  pl.BlockSpec((1, HD), const),
                  pl.BlockSpec((1024, d), const, pipeline_mode=pl.Buffered(1)),
                  pl.BlockSpec((1024, d), const, pipeline_mode=pl.Buffered(1)),
                  pl.BlockSpec((d, d), const, pipeline_mode=pl.Buffered(1))],
        out_specs=pl.BlockSpec((1, tm, d), tile),
        out_shape=jax.ShapeDtypeStruct((b, t, d), F32),
        scratch_shapes=[pltpu.VMEM((tm, 1024), BF16)],
        compiler_params=_params(2),
        name="merge",
    )(o_f, o_b, p, b_lat, p, p, x, mod3, a_norm_w, wpa, wpb, wo)


def _rope_tables():
    t = jnp.arange(SEQ)
    row = (t // GRID_W).astype(F32)
    col = (t % GRID_W).astype(F32)
    n_freq = HD // 4
    inv = ROPE_THETA ** (-jnp.arange(n_freq, dtype=F32) / n_freq)
    ang = jnp.concatenate([row[:, None] * inv, col[:, None] * inv], axis=-1)
    cos, sin = jnp.cos(ang), jnp.sin(ang)
    return jnp.concatenate([cos, cos], axis=-1), jnp.concatenate([-sin, sin], axis=-1)


def _gate_row(a):
    return jnp.zeros((1, LANES), F32).at[0, 2 * HEADS:4 * HEADS].set(a.reshape(-1).astype(F32))


def _layer(x, ctx, c, c_ctx, w_mod, b_mod, norm_w, w_in_t, conv_w, a_log, dt_bias,
           a_norm_w, q_norm_w, k_norm_w, w_proj_a, w_proj_b, w_out):
    b = x.shape[0]
    d = D_MODEL
    w_all = _wprep_call(w_in_t)
    wpa = w_proj_a.astype(BF16)
    wpb = w_proj_b.astype(BF16)
    wo = w_out.astype(BF16)

    cc = jnp.concatenate([c, c_ctx[None, :], jnp.zeros((8 - b - 1, d), F32)], axis=0)
    mod3 = _mod_call(cc, w_mod, b_mod[None, :]).reshape(8, 1, 3 * d)

    nw = norm_w[None, :]
    p = _inproj_call(x, mod3, nw, w_all, mod_row=lambda bi: bi, w_col=lambda j: j,
                     n_col=P_COLS // INPROJ_TN, tn=INPROJ_TN, name="inproj_latent")
    small_blk = COL_SMALL // 1024
    p_ctx = _inproj_call(ctx.reshape(1, b * CTX_LEN, d), mod3, nw, w_all, mod_row=lambda bi: b,
                         w_col=lambda j: jnp.where(j < 3, j, small_blk), n_col=CTX_COLS // 1024,
                         tn=1024, name="inproj_ctx").reshape(b, CTX_LEN, CTX_COLS)

    cosf, sinf = _rope_tables()
    alog_row, dtb_row, knw = _gate_row(a_log), _gate_row(dt_bias), k_norm_w[None, :]
    g, bg, kn = _prep_call(p, conv_w, alog_row, dtb_row, knw, cosf, sinf, latent=True)
    g_ctx, bg_ctx, kn_ctx = _prep_call(p_ctx, conv_w, alog_row, dtb_row, knw, None, None, latent=False)
    states = _gdn_call(g_ctx, bg_ctx)
    o_f, o_b = _gdn_call(g, bg, states)
    b_lat = _attn_call(p, p_ctx, kn, kn_ctx, q_norm_w[None, :], cosf, sinf)
    return _merge_call(o_f, o_b, p, b_lat, x, mod3, a_norm_w[None, :], wpa, wpb, wo)


def kernel(x, c, ctx, c_ctx, w_mod, b_mod, norm_w, w_in, conv_w, a_log, dt_bias, a_norm_w,
           q_norm_w, k_norm_w, w_proj_a, w_proj_b, w_out):
    assert w_mod.shape[0] == 1
    w_in_t = jnp.swapaxes(w_in, 1, 2).reshape(w_in.shape[2], w_in.shape[1])
    return _layer(x, ctx, c, c_ctx, w_mod[0], b_mod[0], norm_w[0], w_in_t, conv_w[0], a_log[0],
                  dt_bias[0], a_norm_w[0], q_norm_w[0], k_norm_w[0], w_proj_a[0], w_proj_b[0],
                  w_out[0])
```

```python
import functools
import math

import jax
import jax.numpy as jnp
from jax import lax
from jax.experimental import pallas as pl
from jax.experimental.pallas import tpu as pltpu

F32 = jnp.float32
BF16 = jnp.bfloat16

D_MODEL = 2048
SEQ = 2048
CTX_LEN = 256
GRID_W = 64
EPS = 1e-6
HEADS = 8
HD = 128
KV_HEADS = 2
A_CONV = 5
ROPE_THETA = 10000.0
CHUNK = 64

COL_AQ, COL_AK, COL_AV, COL_AZ, COL_BQ, COL_BZ, COL_BRA, COL_BRB, COL_SMALL = (
    0, 1024, 2048, 3072, 4096, 5120, 6144, 8192, 10240)
P_COLS = 11264
SMALL_BK, SMALL_BV, SMALL_BA = 0, 256, 512
CTX_COLS = 4096
CTX_COL_SMALL = 3072

VMEM_LIMIT = 56 * 1024 * 1024
LANES = 128
INPROJ_TN = 1024


def _silu(x):
    return x * jax.nn.sigmoid(x)


def _softplus(x):
    return jnp.maximum(x, 0.0) + jnp.log1p(jnp.exp(-jnp.abs(x)))


def _bdot(a, b):
    return jnp.dot(a.astype(BF16), b.astype(BF16), preferred_element_type=F32)


def _bdot_nt(a, b):
    return lax.dot_general(a.astype(BF16), b.astype(BF16), (((1,), (1,)), ((), ())),
                           preferred_element_type=F32)


def _bdot_tn(a, b):
    return lax.dot_general(a.astype(BF16), b.astype(BF16), (((0,), (0,)), ((), ())),
                           preferred_element_type=F32)


def _params(n_axes):
    return pltpu.CompilerParams(dimension_semantics=("arbitrary",) * n_axes,
                                vmem_limit_bytes=VMEM_LIMIT)


WP_ROWS = 512
WP_SHIFT = 32
WP_NBLK = P_COLS // WP_ROWS


def _wprep_src(n):
    return jnp.where(n < 10, n, jnp.where(n < 20, n + 1, jnp.where(n == 20, 10, 8)))


def _wprep_kernel(main_ref, extra_ref, o_ref):
    n = pl.program_id(0)
    keep = WP_ROWS - WP_SHIFT

    @pl.when(n < 8)
    def _():
        o_ref[...] = main_ref[...].astype(BF16)

    @pl.when(jnp.logical_and(n >= 8, n < WP_NBLK - 1))
    def _():
        o_ref[:keep] = main_ref[WP_SHIFT:].astype(BF16)
        o_ref[keep:] = extra_ref[...].astype(BF16)

    @pl.when(n == WP_NBLK - 1)
    def _():
        o_ref[:WP_SHIFT] = main_ref[:WP_SHIFT].astype(BF16)
        o_ref[WP_SHIFT:] = jnp.zeros((keep, o_ref.shape[1]), BF16)


def _wprep_call(w_t):
    d = w_t.shape[1]
    ratio = WP_ROWS // WP_SHIFT
    return pl.pallas_call(
        _wprep_kernel,
        grid=(WP_NBLK,),
        in_specs=[pl.BlockSpec((WP_ROWS, d), lambda n: (_wprep_src(n), 0)),
                  pl.BlockSpec((WP_SHIFT, d), lambda n: (ratio * _wprep_src(n) + ratio, 0))],
        out_specs=pl.BlockSpec((WP_ROWS, d), lambda n: (n, 0)),
        out_shape=jax.ShapeDtypeStruct((P_COLS, d), BF16),
        compiler_params=_params(1),
        name="wprep",
    )(w_t, w_t)


def _mod_kernel(c_ref, w_ref, b_ref, o_ref):
    o_ref[...] = _bdot(_silu(c_ref[...]), w_ref[...]) + b_ref[...]


def _mod_call(cc, w_mod, b_mod):
    tn = 768
    n = w_mod.shape[1]
    return pl.pallas_call(
        _mod_kernel,
        grid=(n // tn,),
        in_specs=[pl.BlockSpec((8, D_MODEL), lambda j: (0, 0)),
                  pl.BlockSpec((D_MODEL, tn), lambda j: (0, j)),
                  pl.BlockSpec((1, tn), lambda j: (0, j))],
        out_specs=pl.BlockSpec((8, tn), lambda j: (0, j)),
        out_shape=jax.ShapeDtypeStruct((8, n), F32),
        compiler_params=_params(1),
        name="mod",
    )(cc, w_mod, b_mod)


def _inproj_kernel(x_ref, shift_ref, scale_ref, nw_ref, w_ref, o_ref, h_ref):
    @pl.when(pl.program_id(2) == 0)
    def _():
        x = x_ref[0]
        r = lax.rsqrt(jnp.mean(x * x, axis=-1, keepdims=True) + EPS)
        gain = nw_ref[...] * (1.0 + scale_ref[0])
        h_ref[...] = ((x * r) * gain + shift_ref[0]).astype(BF16)

    o_ref[0] = _bdot_nt(h_ref[...], w_ref[...]).astype(BF16)


def _inproj_call(x, mod3, norm_w, w_all, *, mod_row, w_col, n_col, tn, name):
    nb, t, d = x.shape
    tm = 1024
    return pl.pallas_call(
        _inproj_kernel,
        grid=(nb, t // tm, n_col),
        in_specs=[pl.BlockSpec((1, tm, d), lambda bi, i, j: (bi, i, 0)),
                  pl.BlockSpec((1, 1, d), lambda bi, i, j: (mod_row(bi), 0, 0)),
                  pl.BlockSpec((1, 1, d), lambda bi, i, j: (mod_row(bi), 0, 1)),
                  pl.BlockSpec((1, d), lambda bi, i, j: (0, 0)),
                  pl.BlockSpec((tn, d), lambda bi, i, j: (w_col(j), 0))],
        out_specs=pl.BlockSpec((1, tm, tn), lambda bi, i, j: (bi, i, j)),
        out_shape=jax.ShapeDtypeStruct((nb, t, n_col * tn), BF16),
        scratch_shapes=[pltpu.VMEM((tm, d), BF16)],
        compiler_params=_params(3),
        name=name,
    )(x, mod3, mod3, norm_w, w_all)


PREP_T = 256
HALO = 16


CONV_OFFS = (-2, -1, 1, 2)


def _prep_kernel(*refs, latent):
    if latent:
        (main_ref, prev_ref, next_ref, small_ref, shift_ref, convw_ref, alog_ref, dtb_ref, knw_ref,
         cos_ref, sin_ref, g_ref, bg_ref, kn_ref, vx_ref) = refs
        i = pl.program_id(1)
        has_prev = i > 0
        has_next = i < pl.num_programs(1) - 1
        row8 = lax.broadcasted_iota(jnp.int32, (8, 1024), 0)
    else:
        (main_ref, small_ref, shift_ref, convw_ref, alog_ref, dtb_ref, knw_ref,
         g_ref, bg_ref, kn_ref, vx_ref) = refs
    t = PREP_T
    for c in range(3):
        cs = slice(c * 1024, (c + 1) * 1024)
        w = [convw_ref[k:k + 1, cs] for k in range(A_CONV)]
        x = main_ref[0, :, cs]
        y = jnp.dot(shift_ref[...], x, preferred_element_type=F32)
        acc = x.astype(F32) * w[2]
        for n, k in enumerate((0, 1, 3, 4)):
            acc = acc + y[n * t:(n + 1) * t] * w[k]
        if latent:
            pv = jnp.where(has_prev, prev_ref[0, HALO - 8:, cs].astype(F32), 0.0)
            nx = jnp.where(has_next, next_ref[0, :8, cs].astype(F32), 0.0)
            top = (jnp.where(row8 < 2, pltpu.roll(pv, 2, 0), 0.0) * w[0]
                   + jnp.where(row8 < 1, pltpu.roll(pv, 1, 0), 0.0) * w[1])
            bot = (jnp.where(row8 >= 7, pltpu.roll(nx, 7, 0), 0.0) * w[3]
                   + jnp.where(row8 >= 6, pltpu.roll(nx, 6, 0), 0.0) * w[4])
            acc = jnp.concatenate([acc[:8] + top, acc[8:t - 8], acc[t - 8:] + bot], axis=0)
        act = _silu(acc)
        if c == 2:
            g_ref[0, :, cs] = act.astype(BF16)
        else:
            mul = HD ** -0.5 if c == 0 else 1.0
            for h in range(HEADS):
                a = act[:, h * HD:(h + 1) * HD]
                r = lax.rsqrt(jnp.sum(a * a, axis=-1, keepdims=True) + EPS)
                g_ref[0, :, c * 1024 + h * HD:c * 1024 + (h + 1) * HD] = (a * (r * mul)).astype(BF16)

    ba = small_ref[0, :, SMALL_BA:SMALL_BA + LANES].astype(F32)
    beta = jax.nn.sigmoid(ba)
    gdec = -jnp.exp(alog_ref[...]) * _softplus(ba + dtb_ref[...])
    lane = lax.broadcasted_iota(jnp.int32, ba.shape, 1)
    bg_ref[0] = jnp.where(lane < 2 * HEADS, beta, gdec)

    for h in range(KV_HEADS):
        kh = small_ref[0, :, SMALL_BK + h * HD:SMALL_BK + (h + 1) * HD].astype(F32)
        r = lax.rsqrt(jnp.mean(kh * kh, axis=-1, keepdims=True) + EPS)
        kn = kh * r * knw_ref[...]
        if latent:
            kn = kn * cos_ref[...] + pltpu.roll(kn, HD // 2, 1) * sin_ref[...]
        kn_ref[0, :, h * HD:(h + 1) * HD] = kn.astype(BF16)
        vx_ref[0, :, 2 * h * HD:(2 * h + 1) * HD] = small_ref[0, :, SMALL_BV + h * HD:SMALL_BV + (h + 1) * HD]
        vx_ref[0, :, (2 * h + 1) * HD:(2 * h + 2) * HD] = jnp.ones((PREP_T, HD), BF16)


def _prep_call(p, conv_w, alog_row, dtb_row, k_norm_w, cosf, sinf, *, latent):
    b, t, _ = p.shape
    nt = t // PREP_T
    const = lambda bi, i: (0, 0)
    tile = lambda bi, i: (bi, i, 0)
    if latent:
        hb = PREP_T // HALO
        last_hb = t // HALO - 1
        small_col = COL_SMALL // 1024
        in_specs = [pl.BlockSpec((1, PREP_T, 3072), tile),
                    pl.BlockSpec((1, HALO, 3072), lambda bi, i: (bi, jnp.maximum(i * hb - 1, 0), 0)),
                    pl.BlockSpec((1, HALO, 3072), lambda bi, i: (bi, jnp.minimum((i + 1) * hb, last_hb), 0)),
                    pl.BlockSpec((1, PREP_T, 1024), lambda bi, i: (bi, i, small_col))]
        args = [p, p, p, p]
    else:
        small_col = CTX_COL_SMALL // 1024
        in_specs = [pl.BlockSpec((1, PREP_T, 3072), tile),
                    pl.BlockSpec((1, PREP_T, 1024), lambda bi, i: (bi, i, small_col))]
        args = [p, p]
    shifts = jnp.concatenate([jnp.eye(PREP_T, k=off, dtype=BF16) for off in CONV_OFFS], axis=0)
    in_specs += [pl.BlockSpec(shifts.shape, const),
                 pl.BlockSpec((A_CONV, 3072), const), pl.BlockSpec((1, LANES), const),
                 pl.BlockSpec((1, LANES), const), pl.BlockSpec((1, HD), const)]
    args += [shifts, conv_w, alog_row, dtb_row, k_norm_w]
    if latent:
        in_specs += [pl.BlockSpec((PREP_T, HD), lambda bi, i: (i, 0))] * 2
        args += [cosf, sinf]
    return pl.pallas_call(
        functools.partial(_prep_kernel, latent=latent),
        grid=(b, nt),
        in_specs=in_specs,
        out_specs=[pl.BlockSpec((1, PREP_T, 3072), tile),
                   pl.BlockSpec((1, PREP_T, LANES), tile),
                   pl.BlockSpec((1, PREP_T, KV_HEADS * HD), tile),
                   pl.BlockSpec((1, PREP_T, 2 * KV_HEADS * HD), tile)],
        out_shape=[jax.ShapeDtypeStruct((b, t, 3072), BF16),
                   jax.ShapeDtypeStruct((b, t, LANES), F32),
                   jax.ShapeDtypeStruct((b, t, KV_HEADS * HD), BF16),
                   jax.ShapeDtypeStruct((b, t, 2 * KV_HEADS * HD), BF16)],
        compiler_params=_params(2),
        name="prep_latent" if latent else "prep_ctx",
    )(*args)


def _chunk_masks(direction):
    c = CHUNK
    ii = lax.broadcasted_iota(jnp.int32, (c, c), 0)
    jj = lax.broadcasted_iota(jnp.int32, (c, c), 1)
    if direction == 0:
        return ii >= jj, ii > jj, c - 1
    return ii <= jj, ii < jj, 0


def _chunk_cumsum(bg, incl):
    tri = incl.astype(BF16)
    g1 = bg.astype(BF16)
    r1 = bg - g1.astype(F32)
    g2 = r1.astype(BF16)
    g3 = (r1 - g2.astype(F32)).astype(BF16)
    return (jnp.dot(tri, g1, preferred_element_type=F32) + jnp.dot(tri, g2, preferred_element_type=F32)
            + jnp.dot(tri, g3, preferred_element_type=F32))


def _gdn_kernel(*refs, latent):
    step = pl.program_id(1)
    if latent:
        qkv_f_ref, bg_f_ref, qkv_b_ref, bg_b_ref, s0_f_ref, s0_b_ref, o_f_ref, o_b_ref, s_f_ref, s_b_ref = refs

        @pl.when(step == 0)
        def _():
            s_f_ref[...] = s0_f_ref[...]
            s_b_ref[...] = s0_b_ref[...]

        o_refs = (o_f_ref, o_b_ref)
    else:
        qkv_f_ref, bg_f_ref, qkv_b_ref, bg_b_ref, s_f_ref, s_b_ref = refs

        @pl.when(step == 0)
        def _():
            s_f_ref[...] = jnp.zeros(s_f_ref.shape, F32)
            s_b_ref[...] = jnp.zeros(s_b_ref.shape, F32)

    state = lambda n, d, h: (s_f_ref, s_b_ref)[d].at[n, h]
    c = CHUNK
    qkv_refs = (qkv_f_ref, qkv_b_ref)
    rows = range(qkv_f_ref.shape[0])
    chains = [(n, d, h) for n in rows for d in range(2) for h in range(HEADS)]

    masks = [_chunk_masks(d) for d in range(2)]
    bg_refs = (bg_f_ref, bg_b_ref)
    bgs = {(n, d): bg_refs[d][n] for n in rows for d in range(2)}
    gcs = {nd: _chunk_cumsum(bgs[nd], masks[nd[1]][0]) for nd in bgs}
    gcts = {nd: jnp.concatenate([gcs[nd], jnp.zeros((LANES - c, LANES), F32)], axis=0).T for nd in bgs}

    q, k, v, kf, beta, gcol, glast, decay = {}, {}, {}, {}, {}, {}, {}, {}
    for ch in chains:
        n, d, h = ch
        incl, _, last = masks[d]
        cb = d * HEADS + h
        cg = 2 * HEADS + cb
        q[ch] = qkv_refs[d][n, :, h * HD:(h + 1) * HD]
        k[ch] = qkv_refs[d][n, :, 1024 + h * HD:1024 + (h + 1) * HD]
        v[ch] = qkv_refs[d][n, :, 2048 + h * HD:2048 + (h + 1) * HD]
        kf[ch] = k[ch].astype(F32)
        beta[ch] = bgs[n, d][:, cb:cb + 1]
        gcol[ch] = gcs[n, d][:, cg:cg + 1]
        grow = gcts[n, d][cg:cg + 1, 0:c]
        glast[ch] = gcs[n, d][last:last + 1, cg:cg + 1]
        decay[ch] = jnp.where(incl, jnp.exp(jnp.where(incl, gcol[ch] - grow, 0.0)), 0.0)

    if latent:
        qk = {ch: _bdot_nt(q[ch], k[ch]) for ch in chains}
        attn = {ch: qk[ch] * decay[ch] for ch in chains}

    gp = GDN_PACK
    groups = [(n, d, hg) for n in rows for d in range(2) for hg in range(HEADS // gp)]
    members = {gr: [(gr[0], gr[1], gr[2] * gp + m) for m in range(gp)] for gr in groups}
    lane_c = lax.broadcasted_iota(jnp.int32, (c, gp * c), 1)
    row_c = lax.broadcasted_iota(jnp.int32, (c, gp * c), 0)
    eye4 = (lane_c % c == row_c).astype(F32)
    blk_c = [(lane_c // c == m).astype(BF16) for m in range(gp)]
    lane_k = lax.broadcasted_iota(jnp.int32, (c, gp * HD), 1)
    blk_k = [(lane_k // HD == m).astype(BF16) for m in range(gp)]

    def bd(x4):
        xb = x4.astype(BF16)
        return jnp.concatenate([xb * blk_c[m] for m in range(gp)], axis=0)

    def pdot(a, b):
        return jnp.dot(a.astype(BF16), b, preferred_element_type=F32)

    k4 = {(n, d, hg): qkv_refs[d][n, :, 1024 + hg * gp * HD:1024 + (hg + 1) * gp * HD]
          for n, d, hg in groups}
    kk4 = {gr: _bdot_nt(k4[gr], jnp.concatenate([k4[gr] * blk_k[m] for m in range(gp)], axis=0))
           for gr in groups}
    l4 = {gr: kk4[gr] * jnp.concatenate(
        [beta[ch] * jnp.where(masks[ch[1]][1], decay[ch], 0.0) for ch in members[gr]], axis=1)
        for gr in groups}

    t4 = {gr: eye4 - l4[gr] for gr in groups}
    lp4 = {gr: pdot(l4[gr], bd(l4[gr])) for gr in groups}
    n_sq = c.bit_length() - 2
    for lvl in range(n_sq):
        last_lvl = lvl == n_sq - 1
        lhs4 = {gr: t4[gr] if last_lvl else jnp.concatenate([t4[gr], lp4[gr]], axis=0) for gr in groups}
        res = {gr: pdot(lhs4[gr], bd(lp4[gr])) for gr in groups}
        t4 = {gr: t4[gr] + res[gr][:c] for gr in groups}
        if not last_lvl:
            lp4 = {gr: res[gr][c:] for gr in groups}

    t0 = {gr: t4[gr].astype(BF16) for gr in groups}
    a4 = {gr: eye4 + l4[gr] for gr in groups}
    a_hi = {gr: a4[gr].astype(BF16) for gr in groups}
    a_lo = {gr: (a4[gr] - a_hi[gr].astype(F32)).astype(BF16) for gr in groups}
    bd_t0 = {gr: bd(t0[gr]) for gr in groups}
    at = {gr: pdot(jnp.concatenate([a_hi[gr], a_lo[gr]], axis=0), bd_t0[gr]) for gr in groups}
    resid = {gr: eye4 - (at[gr][:c] + at[gr][c:]) for gr in groups}
    t1 = {gr: t0[gr].astype(F32) + pdot(t0[gr], bd(resid[gr])) for gr in groups}

    eg = {ch: jnp.exp(gcol[ch]) for ch in chains}
    rhs = {ch: jnp.concatenate([v[ch].astype(F32) * beta[ch], kf[ch] * (beta[ch] * eg[ch])], axis=1)
           for ch in chains}
    rhs4 = {gr: jnp.concatenate([rhs[ch] for ch in members[gr]], axis=0).astype(BF16) for gr in groups}
    uw4 = {gr: jnp.dot(bd(t1[gr]), rhs4[gr], preferred_element_type=F32) for gr in groups}
    uw = {ch: uw4[gr][m * c:(m + 1) * c] for gr in groups for m, ch in enumerate(members[gr])}
    s = {ch: state(*ch)[...] for ch in chains}
    if latent:
        lhs = {ch: jnp.concatenate([uw[ch][:, HD:], q[ch].astype(F32) * eg[ch]], axis=0) for ch in chains}
    else:
        lhs = {ch: uw[ch][:, HD:] for ch in chains}
    ws_qs = {ch: _bdot(lhs[ch], s[ch]) for ch in chains}
    v_new = {ch: uw[ch][:, :HD] - ws_qs[ch][:c] for ch in chains}
    k_dec = {ch: kf[ch] * jnp.exp(glast[ch] - gcol[ch]) for ch in chains}
    kv = {ch: _bdot_tn(k_dec[ch], v_new[ch]) for ch in chains}
    if latent:
        av = {ch: _bdot(attn[ch], v_new[ch]) for ch in chains}
    for ch in chains:
        n, d, h = ch
        if latent:
            o_refs[d][n, :, h * HD:(h + 1) * HD] = (ws_qs[ch][c:] + av[ch]).astype(BF16)
        state(*ch)[...] = s[ch] * jnp.exp(glast[ch]) + kv[ch]


GDN_ROWS = 4
GDN_PACK = 4


def _gdn_call(g, bg, states=None):
    b, t, _ = g.shape
    nc = t // CHUNK
    nr = GDN_ROWS
    latent = states is not None
    fwd = lambda bi, s: (bi, s, 0)
    bwd = lambda bi, s: (bi, nc - 1 - s, 0)
    in_specs = [pl.BlockSpec((nr, CHUNK, 3072), fwd), pl.BlockSpec((nr, CHUNK, LANES), fwd),
                pl.BlockSpec((nr, CHUNK, 3072), bwd), pl.BlockSpec((nr, CHUNK, LANES), bwd)]
    state_spec = pl.BlockSpec((nr, HEADS, HD, HD), lambda bi, s: (bi, 0, 0, 0))
    state_shape = jax.ShapeDtypeStruct((b, HEADS, HD, HD), F32)
    if latent:
        return pl.pallas_call(
            functools.partial(_gdn_kernel, latent=True),
            grid=(b // nr, nc),
            in_specs=in_specs + [state_spec, state_spec],
            out_specs=[pl.BlockSpec((nr, CHUNK, HEADS * HD), fwd),
                       pl.BlockSpec((nr, CHUNK, HEADS * HD), bwd)],
            out_shape=[jax.ShapeDtypeStruct((b, t, HEADS * HD), BF16)] * 2,
            scratch_shapes=[pltpu.VMEM((nr, HEADS, HD, HD), F32)] * 2,
            compiler_params=_params(2),
            name="gdn_latent",
        )(g, bg, g, bg, *states)
    return pl.pallas_call(
        functools.partial(_gdn_kernel, latent=False),
        grid=(b // nr, nc),
        in_specs=in_specs,
        out_specs=[state_spec, state_spec],
        out_shape=[state_shape, state_shape],
        compiler_params=_params(2),
        name="gdn_ctx",
    )(g, bg, g, bg)


ATT_TQ = 256
Q_PER_KV = HEADS // KV_HEADS
Q_SCALE = HD ** -0.5 * math.log2(math.e)


def _attn_kernel(q_ref, z_ref, kl_ref, kc_ref, vl_ref, vc_ref, qnw_ref, cos_ref, sin_ref, o_ref):
    def scores(j):
        g = slice((j // Q_PER_KV) * HD, (j // Q_PER_KV + 1) * HD)
        qh = q_ref[0, :, j * HD:(j + 1) * HD].astype(F32)
        r = lax.rsqrt(jnp.mean(qh * qh, axis=-1, keepdims=True) + EPS)
        qn = qh * r * qnw_ref[...]
        qr = (qn * cos_ref[...] + pltpu.roll(qn, HD // 2, 1) * sin_ref[...]) * Q_SCALE
        return _bdot_nt(qr, kl_ref[0, :, g]), _bdot_nt(qr, kc_ref[0, :, g])

    nxt = scores(0)
    for j in range(HEADS):
        sl = slice(j * HD, (j + 1) * HD)
        gx = slice((j // Q_PER_KV) * 2 * HD, (j // Q_PER_KV + 1) * 2 * HD)
        s_l, s_c = nxt
        if j + 1 < HEADS:
            nxt = scores(j + 1)
        m = jnp.maximum(jnp.max(s_l, axis=-1, keepdims=True), jnp.max(s_c, axis=-1, keepdims=True))
        e_l = jnp.exp2(s_l - m).astype(BF16)
        e_c = jnp.exp2(s_c - m).astype(BF16)
        ox = (jnp.dot(e_l, vl_ref[0, :, gx], preferred_element_type=F32)
              + jnp.dot(e_c, vc_ref[0, :, gx], preferred_element_type=F32))
        o = ox[:, :HD] / ox[:, HD:]
        o_ref[0, :, sl] = (o * _silu(z_ref[0, :, sl].astype(F32))).astype(BF16)


def _attn_call(p, kn, kn_ctx, vx, vx_ctx, q_norm_w, cosf, sinf):
    b = p.shape[0]
    qw = HEADS * HD
    kvw = KV_HEADS * HD
    return pl.pallas_call(
        _attn_kernel,
        grid=(b, SEQ // ATT_TQ),
        in_specs=[pl.BlockSpec((1, ATT_TQ, qw), lambda bi, i: (bi, i, COL_BQ // qw)),
                  pl.BlockSpec((1, ATT_TQ, qw), lambda bi, i: (bi, i, COL_BZ // qw)),
                  pl.BlockSpec((1, SEQ, kvw), lambda bi, i: (bi, 0, 0)),
                  pl.BlockSpec((1, CTX_LEN, kvw), lambda bi, i: (bi, 0, 0)),
                  pl.BlockSpec((1, SEQ, 2 * kvw), lambda bi, i: (bi, 0, 0)),
                  pl.BlockSpec((1, CTX_LEN, 2 * kvw), lambda bi, i: (bi, 0, 0)),
                  pl.BlockSpec((1, HD), lambda bi, i: (0, 0)),
                  pl.BlockSpec((ATT_TQ, HD), lambda bi, i: (i, 0)),
                  pl.BlockSpec((ATT_TQ, HD), lambda bi, i: (i, 0))],
        out_specs=pl.BlockSpec((1, ATT_TQ, qw), lambda bi, i: (bi, i, 0)),
        out_shape=jax.ShapeDtypeStruct((b, SEQ, qw), BF16),
        compiler_params=_params(2),
        name="attn",
    )(p, p, kn, kn_ctx, vx, vx_ctx, q_norm_w, cosf, sinf)


MERGE_T = 512


def _merge_kernel(of_ref, ob_ref, az_ref, bl_ref, bra_ref, brb_ref, x_ref, gate_ref, anw_ref,
                  wpa_ref, wpb_ref, wo_ref, out_ref, a_ref):
    o = of_ref[0].astype(F32) + ob_ref[0].astype(F32)
    for h in range(HEADS):
        sl = slice(h * HD, (h + 1) * HD)
        oh = o[:, sl]
        r = lax.rsqrt(jnp.mean(oh * oh, axis=-1, keepdims=True) + EPS)
        a_ref[:, sl] = (oh * r * anw_ref[...] * _silu(az_ref[0, :, sl].astype(F32))).astype(BF16)
    pa = jnp.dot(a_ref[...], wpa_ref[...], preferred_element_type=F32)
    pb = jnp.dot(bl_ref[0], wpb_ref[...], preferred_element_type=F32)
    m = (jax.nn.sigmoid(bra_ref[0].astype(F32)) * pa + jax.nn.sigmoid(brb_ref[0].astype(F32)) * pb)
    y = jnp.dot(m.astype(BF16), wo_ref[...], preferred_element_type=F32)
    out_ref[0] = x_ref[0] + gate_ref[0] * y


def _merge_call(o_f, o_b, p, b_lat, x, mod3, a_norm_w, wpa, wpb, wo):
    b, t, d = x.shape
    tm = MERGE_T
    tile = lambda bi, i: (bi, i, 0)
    const = lambda bi, i: (0, 0)
    return pl.pallas_call(
        _merge_kernel,
        grid=(b, t // tm),
        in_specs=[pl.BlockSpec((1, tm, 1024), tile),
                  pl.BlockSpec((1, tm, 1024), tile),
                  pl.BlockSpec((1, tm, 1024), lambda bi, i: (bi, i, COL_AZ // 1024)),
                  pl.BlockSpec((1, tm, 1024), tile),
                  pl.BlockSpec((1, tm, d), lambda bi, i: (bi, i, COL_BRA // d)),
                  pl.BlockSpec((1, tm, d), lambda bi, i: (bi, i, COL_BRB // d)),
                  pl.BlockSpec((1, tm, d), tile),
                  pl.BlockSpec((1, 1, d), lambda bi, i: (bi, 0, 2)),
                  pl.BlockSpec((1, HD), const),
                  pl.BlockSpec((1024, d), const, pipeline_mode=pl.Buffered(1)),
                  pl.BlockSpec((1024, d), const, pipeline_mode=pl.Buffered(1)),
                  pl.BlockSpec((d, d), const, pipeline_mode=pl.Buffered(1))],
        out_specs=pl.BlockSpec((1, tm, d), tile),
        out_shape=jax.ShapeDtypeStruct((b, t, d), F32),
        scratch_shapes=[pltpu.VMEM((tm, 1024), BF16)],
        compiler_params=_params(2),
        name="merge",
    )(o_f, o_b, p, b_lat, p, p, x, mod3, a_norm_w, wpa, wpb, wo)


def _rope_tables():
    t = jnp.arange(SEQ)
    row = (t // GRID_W).astype(F32)
    col = (t % GRID_W).astype(F32)
    n_freq = HD // 4
    inv = ROPE_THETA ** (-jnp.arange(n_freq, dtype=F32) / n_freq)
    ang = jnp.concatenate([row[:, None] * inv, col[:, None] * inv], axis=-1)
    cos, sin = jnp.cos(ang), jnp.sin(ang)
    return jnp.concatenate([cos, cos], axis=-1), jnp.concatenate([-sin, sin], axis=-1)


def _gate_row(a):
    return jnp.zeros((1, LANES), F32).at[0, 2 * HEADS:4 * HEADS].set(a.reshape(-1).astype(F32))


def _layer(x, ctx, c, c_ctx, w_mod, b_mod, norm_w, w_in_t, conv_w, a_log, dt_bias,
           a_norm_w, q_norm_w, k_norm_w, w_proj_a, w_proj_b, w_out):
    b = x.shape[0]
    d = D_MODEL
    w_all = _wprep_call(w_in_t)
    wpa = w_proj_a.astype(BF16)
    wpb = w_proj_b.astype(BF16)
    wo = w_out.astype(BF16)

    cc = jnp.concatenate([c, c_ctx[None, :], jnp.zeros((8 - b - 1, d), F32)], axis=0)
    mod3 = _mod_call(cc, w_mod, b_mod[None, :]).reshape(8, 1, 3 * d)

    nw = norm_w[None, :]
    p = _inproj_call(x, mod3, nw, w_all, mod_row=lambda bi: bi, w_col=lambda j: j,
                     n_col=P_COLS // INPROJ_TN, tn=INPROJ_TN, name="inproj_latent")
    small_blk = COL_SMALL // 1024
    p_ctx = _inproj_call(ctx.reshape(1, b * CTX_LEN, d), mod3, nw, w_all, mod_row=lambda bi: b,
                         w_col=lambda j: jnp.where(j < 3, j, small_blk), n_col=CTX_COLS // 1024,
                         tn=1024, name="inproj_ctx").reshape(b, CTX_LEN, CTX_COLS)

    cosf, sinf = _rope_tables()
    alog_row, dtb_row, knw = _gate_row(a_log), _gate_row(dt_bias), k_norm_w[None, :]
    g, bg, kn, vx = _prep_call(p, conv_w, alog_row, dtb_row, knw, cosf, sinf, latent=True)
    g_ctx, bg_ctx, kn_ctx, vx_ctx = _prep_call(p_ctx, conv_w, alog_row, dtb_row, knw, None, None,
                                               latent=False)
    states = _gdn_call(g_ctx, bg_ctx)
    o_f, o_b = _gdn_call(g, bg, states)
    b_lat = _attn_call(p, kn, kn_ctx, vx, vx_ctx, q_norm_w[None, :], cosf, sinf)
    return _merge_call(o_f, o_b, p, b_lat, x, mod3, a_norm_w[None, :], wpa, wpb, wo)


def kernel(x, c, ctx, c_ctx, w_mod, b_mod, norm_w, w_in, conv_w, a_log, dt_bias, a_norm_w,
           q_norm_w, k_norm_w, w_proj_a, w_proj_b, w_out):
    assert w_mod.shape[0] == 1
    w_in_t = jnp.swapaxes(w_in, 1, 2).reshape(w_in.shape[2], w_in.shape[1])
    return _layer(x, ctx, c, c_ctx, w_mod[0], b_mod[0], norm_w[0], w_in_t, conv_w[0], a_log[0],
                  dt_bias[0], a_norm_w[0], q_norm_w[0], k_norm_w[0], w_proj_a[0], w_proj_b[0],
                  w_out[0])
```

```python
import functools
import math

import jax
import jax.numpy as jnp
from jax import lax
from jax.experimental import pallas as pl
from jax.experimental.pallas import tpu as pltpu

F32 = jnp.float32
BF16 = jnp.bfloat16

D_MODEL = 2048
SEQ = 2048
CTX_LEN = 256
GRID_W = 64
EPS = 1e-6
HEADS = 8
HD = 128
KV_HEADS = 2
A_CONV = 5
ROPE_THETA = 10000.0
CHUNK = 64

COL_AQ, COL_AK, COL_AV, COL_AZ, COL_BQ, COL_BZ, COL_BRA, COL_BRB, COL_SMALL = (
    0, 1024, 2048, 3072, 4096, 5120, 6144, 8192, 10240)
P_COLS = 11264
SMALL_BK, SMALL_BV, SMALL_BA = 0, 256, 512
CTX_COLS = 4096
CTX_COL_SMALL = 3072

VMEM_LIMIT = 56 * 1024 * 1024
LANES = 128
INPROJ_TN = 1024


def _silu(x):
    return x * jax.nn.sigmoid(x)


def _softplus(x):
    return jnp.maximum(x, 0.0) + jnp.log1p(jnp.exp(-jnp.abs(x)))


def _bdot(a, b):
    return jnp.dot(a.astype(BF16), b.astype(BF16), preferred_element_type=F32)


def _bdot_nt(a, b):
    return lax.dot_general(a.astype(BF16), b.astype(BF16), (((1,), (1,)), ((), ())),
                           preferred_element_type=F32)


def _bdot_tn(a, b):
    return lax.dot_general(a.astype(BF16), b.astype(BF16), (((0,), (0,)), ((), ())),
                           preferred_element_type=F32)


def _params(n_axes):
    return pltpu.CompilerParams(dimension_semantics=("arbitrary",) * n_axes,
                                vmem_limit_bytes=VMEM_LIMIT)


WP_ROWS = 512
WP_SHIFT = 32
WP_NBLK = P_COLS // WP_ROWS


def _wprep_src(n):
    return jnp.where(n < 10, n, jnp.where(n < 20, n + 1, jnp.where(n == 20, 10, 8)))


def _wprep_kernel(main_ref, extra_ref, o_ref):
    n = pl.program_id(0)
    keep = WP_ROWS - WP_SHIFT

    @pl.when(n < 8)
    def _():
        o_ref[...] = main_ref[...].astype(BF16)

    @pl.when(jnp.logical_and(n >= 8, n < WP_NBLK - 1))
    def _():
        o_ref[:keep] = main_ref[WP_SHIFT:].astype(BF16)
        o_ref[keep:] = extra_ref[...].astype(BF16)

    @pl.when(n == WP_NBLK - 1)
    def _():
        o_ref[:WP_SHIFT] = main_ref[:WP_SHIFT].astype(BF16)
        o_ref[WP_SHIFT:] = jnp.zeros((keep, o_ref.shape[1]), BF16)


def _wprep_call(w_t):
    d = w_t.shape[1]
    ratio = WP_ROWS // WP_SHIFT
    return pl.pallas_call(
        _wprep_kernel,
        grid=(WP_NBLK,),
        in_specs=[pl.BlockSpec((WP_ROWS, d), lambda n: (_wprep_src(n), 0)),
                  pl.BlockSpec((WP_SHIFT, d), lambda n: (ratio * _wprep_src(n) + ratio, 0))],
        out_specs=pl.BlockSpec((WP_ROWS, d), lambda n: (n, 0)),
        out_shape=jax.ShapeDtypeStruct((P_COLS, d), BF16),
        compiler_params=_params(1),
        name="wprep",
    )(w_t, w_t)


def _mod_kernel(c_ref, w_ref, b_ref, o_ref):
    o_ref[...] = _bdot(_silu(c_ref[...]), w_ref[...]) + b_ref[...]


def _mod_call(cc, w_mod, b_mod):
    tn = 768
    n = w_mod.shape[1]
    return pl.pallas_call(
        _mod_kernel,
        grid=(n // tn,),
        in_specs=[pl.BlockSpec((8, D_MODEL), lambda j: (0, 0)),
                  pl.BlockSpec((D_MODEL, tn), lambda j: (0, j)),
                  pl.BlockSpec((1, tn), lambda j: (0, j))],
        out_specs=pl.BlockSpec((8, tn), lambda j: (0, j)),
        out_shape=jax.ShapeDtypeStruct((8, n), F32),
        compiler_params=_params(1),
        name="mod",
    )(cc, w_mod, b_mod)


def _inproj_kernel(x_ref, shift_ref, scale_ref, nw_ref, w_ref, o_ref, h_ref):
    @pl.when(pl.program_id(2) == 0)
    def _():
        x = x_ref[0]
        r = lax.rsqrt(jnp.mean(x * x, axis=-1, keepdims=True) + EPS)
        gain = nw_ref[...] * (1.0 + scale_ref[0])
        h_ref[...] = ((x * r) * gain + shift_ref[0]).astype(BF16)

    o_ref[0] = _bdot_nt(h_ref[...], w_ref[...]).astype(BF16)


def _inproj_call(x, mod3, norm_w, w_all, *, mod_row, w_col, n_col, tn, name):
    nb, t, d = x.shape
    tm = 1024
    return pl.pallas_call(
        _inproj_kernel,
        grid=(nb, t // tm, n_col),
        in_specs=[pl.BlockSpec((1, tm, d), lambda bi, i, j: (bi, i, 0)),
                  pl.BlockSpec((1, 1, d), lambda bi, i, j: (mod_row(bi), 0, 0)),
                  pl.BlockSpec((1, 1, d), lambda bi, i, j: (mod_row(bi), 0, 1)),
                  pl.BlockSpec((1, d), lambda bi, i, j: (0, 0)),
                  pl.BlockSpec((tn, d), lambda bi, i, j: (w_col(j), 0))],
        out_specs=pl.BlockSpec((1, tm, tn), lambda bi, i, j: (bi, i, j)),
        out_shape=jax.ShapeDtypeStruct((nb, t, n_col * tn), BF16),
        scratch_shapes=[pltpu.VMEM((tm, d), BF16)],
        compiler_params=_params(3),
        name=name,
    )(x, mod3, mod3, norm_w, w_all)


PREP_T = 256
HALO = 16


CONV_OFFS = (-2, -1, 1, 2)


def _prep_kernel(*refs, latent):
    if latent:
        (main_ref, prev_ref, next_ref, small_ref, shift_ref, convw_ref, alog_ref, dtb_ref, knw_ref,
         cos_ref, sin_ref, g_ref, bg_ref, kn_ref, vx_ref) = refs
        i = pl.program_id(1)
        has_prev = i > 0
        has_next = i < pl.num_programs(1) - 1
        row8 = lax.broadcasted_iota(jnp.int32, (8, 1024), 0)
    else:
        (main_ref, small_ref, shift_ref, convw_ref, alog_ref, dtb_ref, knw_ref,
         g_ref, bg_ref, kn_ref, vx_ref) = refs
    t = PREP_T
    for c in range(3):
        cs = slice(c * 1024, (c + 1) * 1024)
        w = [convw_ref[k:k + 1, cs] for k in range(A_CONV)]
        x = main_ref[0, :, cs]
        xw = jnp.concatenate([x * w[k].astype(BF16) for k in (0, 1, 3, 4)], axis=0)
        acc = x.astype(F32) * w[2] + jnp.dot(shift_ref[...], xw, preferred_element_type=F32)
        if latent:
            pv = jnp.where(has_prev, prev_ref[0, HALO - 8:, cs].astype(F32), 0.0)
            nx = jnp.where(has_next, next_ref[0, :8, cs].astype(F32), 0.0)
            top = (jnp.where(row8 < 2, pltpu.roll(pv, 2, 0), 0.0) * w[0]
                   + jnp.where(row8 < 1, pltpu.roll(pv, 1, 0), 0.0) * w[1])
            bot = (jnp.where(row8 >= 7, pltpu.roll(nx, 7, 0), 0.0) * w[3]
                   + jnp.where(row8 >= 6, pltpu.roll(nx, 6, 0), 0.0) * w[4])
            acc = jnp.concatenate([acc[:8] + top, acc[8:t - 8], acc[t - 8:] + bot], axis=0)
        act = _silu(acc)
        if c == 2:
            g_ref[0, :, cs] = act.astype(BF16)
        else:
            mul = HD ** -0.5 if c == 0 else 1.0
            for h in range(HEADS):
                a = act[:, h * HD:(h + 1) * HD]
                r = lax.rsqrt(jnp.sum(a * a, axis=-1, keepdims=True) + EPS)
                g_ref[0, :, c * 1024 + h * HD:c * 1024 + (h + 1) * HD] = (a * (r * mul)).astype(BF16)

    ba = small_ref[0, :, SMALL_BA:SMALL_BA + LANES].astype(F32)
    beta = jax.nn.sigmoid(ba)
    gdec = -jnp.exp(alog_ref[...]) * _softplus(ba + dtb_ref[...])
    lane = lax.broadcasted_iota(jnp.int32, ba.shape, 1)
    bg_ref[0] = jnp.where(lane < 2 * HEADS, beta, gdec)

    for h in range(KV_HEADS):
        kh = small_ref[0, :, SMALL_BK + h * HD:SMALL_BK + (h + 1) * HD].astype(F32)
        r = lax.rsqrt(jnp.mean(kh * kh, axis=-1, keepdims=True) + EPS)
        kn = kh * r * knw_ref[...]
        if latent:
            kn = kn * cos_ref[...] + pltpu.roll(kn, HD // 2, 1) * sin_ref[...]
        kn_ref[0, :, h * HD:(h + 1) * HD] = kn.astype(BF16)
        vx_ref[0, :, 2 * h * HD:(2 * h + 1) * HD] = small_ref[0, :, SMALL_BV + h * HD:SMALL_BV + (h + 1) * HD]
        vx_ref[0, :, (2 * h + 1) * HD:(2 * h + 2) * HD] = jnp.ones((PREP_T, HD), BF16)


def _prep_call(p, conv_w, alog_row, dtb_row, k_norm_w, cosf, sinf, *, latent):
    b, t, _ = p.shape
    nt = t // PREP_T
    const = lambda bi, i: (0, 0)
    tile = lambda bi, i: (bi, i, 0)
    if latent:
        hb = PREP_T // HALO
        last_hb = t // HALO - 1
        small_col = COL_SMALL // 1024
        in_specs = [pl.BlockSpec((1, PREP_T, 3072), tile),
                    pl.BlockSpec((1, HALO, 3072), lambda bi, i: (bi, jnp.maximum(i * hb - 1, 0), 0)),
                    pl.BlockSpec((1, HALO, 3072), lambda bi, i: (bi, jnp.minimum((i + 1) * hb, last_hb), 0)),
                    pl.BlockSpec((1, PREP_T, 1024), lambda bi, i: (bi, i, small_col))]
        args = [p, p, p, p]
    else:
        small_col = CTX_COL_SMALL // 1024
        in_specs = [pl.BlockSpec((1, PREP_T, 3072), tile),
                    pl.BlockSpec((1, PREP_T, 1024), lambda bi, i: (bi, i, small_col))]
        args = [p, p]
    shifts = jnp.concatenate([jnp.eye(PREP_T, k=off, dtype=BF16) for off in CONV_OFFS], axis=1)
    in_specs += [pl.BlockSpec(shifts.shape, const),
                 pl.BlockSpec((A_CONV, 3072), const), pl.BlockSpec((1, LANES), const),
                 pl.BlockSpec((1, LANES), const), pl.BlockSpec((1, HD), const)]
    args += [shifts, conv_w, alog_row, dtb_row, k_norm_w]
    if latent:
        in_specs += [pl.BlockSpec((PREP_T, HD), lambda bi, i: (i, 0))] * 2
        args += [cosf, sinf]
    return pl.pallas_call(
        functools.partial(_prep_kernel, latent=latent),
        grid=(b, nt),
        in_specs=in_specs,
        out_specs=[pl.BlockSpec((1, PREP_T, 3072), tile),
                   pl.BlockSpec((1, PREP_T, LANES), tile),
                   pl.BlockSpec((1, PREP_T, KV_HEADS * HD), tile),
                   pl.BlockSpec((1, PREP_T, 2 * KV_HEADS * HD), tile)],
        out_shape=[jax.ShapeDtypeStruct((b, t, 3072), BF16),
                   jax.ShapeDtypeStruct((b, t, LANES), F32),
                   jax.ShapeDtypeStruct((b, t, KV_HEADS * HD), BF16),
                   jax.ShapeDtypeStruct((b, t, 2 * KV_HEADS * HD), BF16)],
        compiler_params=_params(2),
        name="prep_latent" if latent else "prep_ctx",
    )(*args)


def _chunk_masks(direction):
    c = CHUNK
    ii = lax.broadcasted_iota(jnp.int32, (c, c), 0)
    jj = lax.broadcasted_iota(jnp.int32, (c, c), 1)
    if direction == 0:
        return ii >= jj, ii > jj, c - 1
    return ii <= jj, ii < jj, 0


def _chunk_cumsum(bg, incl):
    tri = incl.astype(BF16)
    g1 = bg.astype(BF16)
    r1 = bg - g1.astype(F32)
    g2 = r1.astype(BF16)
    g3 = (r1 - g2.astype(F32)).astype(BF16)
    return (jnp.dot(tri, g1, preferred_element_type=F32) + jnp.dot(tri, g2, preferred_element_type=F32)
            + jnp.dot(tri, g3, preferred_element_type=F32))


def _gdn_kernel(*refs, latent):
    step = pl.program_id(1)
    if latent:
        qkv_f_ref, bg_f_ref, qkv_b_ref, bg_b_ref, s0_f_ref, s0_b_ref, o_f_ref, o_b_ref, s_f_ref, s_b_ref = refs

        @pl.when(step == 0)
        def _():
            s_f_ref[...] = s0_f_ref[...]
            s_b_ref[...] = s0_b_ref[...]

        o_refs = (o_f_ref, o_b_ref)
    else:
        qkv_f_ref, bg_f_ref, qkv_b_ref, bg_b_ref, s_f_ref, s_b_ref = refs

        @pl.when(step == 0)
        def _():
            s_f_ref[...] = jnp.zeros(s_f_ref.shape, F32)
            s_b_ref[...] = jnp.zeros(s_b_ref.shape, F32)

    state = lambda n, d, h: (s_f_ref, s_b_ref)[d].at[n, h]
    c = CHUNK
    qkv_refs = (qkv_f_ref, qkv_b_ref)
    rows = range(qkv_f_ref.shape[0])
    chains = [(n, d, h) for n in rows for d in range(2) for h in range(HEADS)]

    masks = [_chunk_masks(d) for d in range(2)]
    bg_refs = (bg_f_ref, bg_b_ref)
    bgs = {(n, d): bg_refs[d][n] for n in rows for d in range(2)}
    gcs = {nd: _chunk_cumsum(bgs[nd], masks[nd[1]][0]) for nd in bgs}
    gcts = {nd: jnp.concatenate([gcs[nd], jnp.zeros((LANES - c, LANES), F32)], axis=0).T for nd in bgs}

    q, k, v, kf, beta, gcol, glast, decay = {}, {}, {}, {}, {}, {}, {}, {}
    for ch in chains:
        n, d, h = ch
        incl, _, last = masks[d]
        cb = d * HEADS + h
        cg = 2 * HEADS + cb
        q[ch] = qkv_refs[d][n, :, h * HD:(h + 1) * HD]
        k[ch] = qkv_refs[d][n, :, 1024 + h * HD:1024 + (h + 1) * HD]
        v[ch] = qkv_refs[d][n, :, 2048 + h * HD:2048 + (h + 1) * HD]
        kf[ch] = k[ch].astype(F32)
        beta[ch] = bgs[n, d][:, cb:cb + 1]
        gcol[ch] = gcs[n, d][:, cg:cg + 1]
        grow = gcts[n, d][cg:cg + 1, 0:c]
        glast[ch] = gcs[n, d][last:last + 1, cg:cg + 1]
        decay[ch] = jnp.where(incl, jnp.exp(jnp.where(incl, gcol[ch] - grow, 0.0)), 0.0)

    if latent:
        qk = {ch: _bdot_nt(q[ch], k[ch]) for ch in chains}
        attn = {ch: qk[ch] * decay[ch] for ch in chains}

    gp = GDN_PACK
    groups = [(n, d, hg) for n in rows for d in range(2) for hg in range(HEADS // gp)]
    members = {gr: [(gr[0], gr[1], gr[2] * gp + m) for m in range(gp)] for gr in groups}
    lane_c = lax.broadcasted_iota(jnp.int32, (c, gp * c), 1)
    row_c = lax.broadcasted_iota(jnp.int32, (c, gp * c), 0)
    eye4 = ((lane_c & (c - 1)) == row_c).astype(F32)
    blk_c = [((lane_c >> (c.bit_length() - 1)) == m).astype(BF16) for m in range(gp)]
    lane_k = lax.broadcasted_iota(jnp.int32, (c, gp * HD), 1)
    blk_k = [((lane_k >> (HD.bit_length() - 1)) == m).astype(BF16) for m in range(gp)]

    def bd(x4):
        xb = x4.astype(BF16)
        return jnp.concatenate([xb * blk_c[m] for m in range(gp)], axis=0)

    def pdot(a, b):
        return jnp.dot(a.astype(BF16), b, preferred_element_type=F32)

    k4 = {(n, d, hg): qkv_refs[d][n, :, 1024 + hg * gp * HD:1024 + (hg + 1) * gp * HD]
          for n, d, hg in groups}
    kk4 = {gr: _bdot_nt(k4[gr], jnp.concatenate([k4[gr] * blk_k[m] for m in range(gp)], axis=0))
           for gr in groups}
    l4 = {gr: kk4[gr] * jnp.concatenate(
        [beta[ch] * jnp.where(masks[ch[1]][1], decay[ch], 0.0) for ch in members[gr]], axis=1)
        for gr in groups}

    t4 = {gr: eye4 - l4[gr] for gr in groups}
    lp4 = {gr: pdot(l4[gr], bd(l4[gr])) for gr in groups}
    n_sq = c.bit_length() - 2
    for lvl in range(n_sq):
        last_lvl = lvl == n_sq - 1
        lhs4 = {gr: t4[gr] if last_lvl else jnp.concatenate([t4[gr], lp4[gr]], axis=0) for gr in groups}
        res = {gr: pdot(lhs4[gr], bd(lp4[gr])) for gr in groups}
        t4 = {gr: t4[gr] + res[gr][:c] for gr in groups}
        if not last_lvl:
            lp4 = {gr: res[gr][c:] for gr in groups}

    t0 = {gr: t4[gr].astype(BF16) for gr in groups}
    a4 = {gr: eye4 + l4[gr] for gr in groups}
    a_hi = {gr: a4[gr].astype(BF16) for gr in groups}
    a_lo = {gr: (a4[gr] - a_hi[gr].astype(F32)).astype(BF16) for gr in groups}
    bd_t0 = {gr: bd(t0[gr]) for gr in groups}
    at = {gr: pdot(jnp.concatenate([a_hi[gr], a_lo[gr]], axis=0), bd_t0[gr]) for gr in groups}
    resid = {gr: eye4 - (at[gr][:c] + at[gr][c:]) for gr in groups}
    t1 = {gr: t0[gr].astype(F32) + pdot(t0[gr], bd(resid[gr])) for gr in groups}

    eg = {ch: jnp.exp(gcol[ch]) for ch in chains}
    rhs = {ch: jnp.concatenate([v[ch].astype(F32) * beta[ch], kf[ch] * (beta[ch] * eg[ch])], axis=1)
           for ch in chains}
    rhs4 = {gr: jnp.concatenate([rhs[ch] for ch in members[gr]], axis=0).astype(BF16) for gr in groups}
    uw4 = {gr: jnp.dot(bd(t1[gr]), rhs4[gr], preferred_element_type=F32) for gr in groups}
    uw = {ch: uw4[gr][m * c:(m + 1) * c] for gr in groups for m, ch in enumerate(members[gr])}
    s = {ch: state(*ch)[...] for ch in chains}
    if latent:
        lhs = {ch: jnp.concatenate([uw[ch][:, HD:], q[ch].astype(F32) * eg[ch]], axis=0) for ch in chains}
    else:
        lhs = {ch: uw[ch][:, HD:] for ch in chains}
    ws_qs = {ch: _bdot(lhs[ch], s[ch]) for ch in chains}
    v_new = {ch: uw[ch][:, :HD] - ws_qs[ch][:c] for ch in chains}
    k_dec = {ch: kf[ch] * jnp.exp(glast[ch] - gcol[ch]) for ch in chains}
    kv = {ch: _bdot_tn(k_dec[ch], v_new[ch]) for ch in chains}
    if latent:
        av = {ch: _bdot(attn[ch], v_new[ch]) for ch in chains}
    for ch in chains:
        n, d, h = ch
        if latent:
            o_refs[d][n, :, h * HD:(h + 1) * HD] = (ws_qs[ch][c:] + av[ch]).astype(BF16)
        state(*ch)[...] = s[ch] * jnp.exp(glast[ch]) + kv[ch]


GDN_ROWS = 4
GDN_PACK = 4


def _gdn_call(g, bg, states=None):
    b, t, _ = g.shape
    nc = t // CHUNK
    nr = GDN_ROWS
    latent = states is not None
    fwd = lambda bi, s: (bi, s, 0)
    bwd = lambda bi, s: (bi, nc - 1 - s, 0)
    in_specs = [pl.BlockSpec((nr, CHUNK, 3072), fwd), pl.BlockSpec((nr, CHUNK, LANES), fwd),
                pl.BlockSpec((nr, CHUNK, 3072), bwd), pl.BlockSpec((nr, CHUNK, LANES), bwd)]
    state_spec = pl.BlockSpec((nr, HEADS, HD, HD), lambda bi, s: (bi, 0, 0, 0))
    state_shape = jax.ShapeDtypeStruct((b, HEADS, HD, HD), F32)
    if latent:
        return pl.pallas_call(
            functools.partial(_gdn_kernel, latent=True),
            grid=(b // nr, nc),
            in_specs=in_specs + [state_spec, state_spec],
            out_specs=[pl.BlockSpec((nr, CHUNK, HEADS * HD), fwd),
                       pl.BlockSpec((nr, CHUNK, HEADS * HD), bwd)],
            out_shape=[jax.ShapeDtypeStruct((b, t, HEADS * HD), BF16)] * 2,
            scratch_shapes=[pltpu.VMEM((nr, HEADS, HD, HD), F32)] * 2,
            compiler_params=_params(2),
            name="gdn_latent",
        )(g, bg, g, bg, *states)
    return pl.pallas_call(
        functools.partial(_gdn_kernel, latent=False),
        grid=(b // nr, nc),
        in_specs=in_specs,
        out_specs=[state_spec, state_spec],
        out_shape=[state_shape, state_shape],
        compiler_params=_params(2),
        name="gdn_ctx",
    )(g, bg, g, bg)


ATT_TQ = 256
Q_PER_KV = HEADS // KV_HEADS
Q_SCALE = HD ** -0.5 * math.log2(math.e)


def _attn_kernel(q_ref, z_ref, kl_ref, kc_ref, vl_ref, vc_ref, qnw_ref, cos_ref, sin_ref, o_ref):
    def scores(j):
        g = slice((j // Q_PER_KV) * HD, (j // Q_PER_KV + 1) * HD)
        qh = q_ref[0, :, j * HD:(j + 1) * HD].astype(F32)
        r = lax.rsqrt(jnp.mean(qh * qh, axis=-1, keepdims=True) + EPS)
        qn = qh * r * qnw_ref[...]
        qr = (qn * cos_ref[...] + pltpu.roll(qn, HD // 2, 1) * sin_ref[...]) * Q_SCALE
        return _bdot_nt(qr, kl_ref[0, :, g]), _bdot_nt(qr, kc_ref[0, :, g])

    nxt = scores(0)
    for j in range(HEADS):
        sl = slice(j * HD, (j + 1) * HD)
        gx = slice((j // Q_PER_KV) * 2 * HD, (j // Q_PER_KV + 1) * 2 * HD)
        s_l, s_c = nxt
        if j + 1 < HEADS:
            nxt = scores(j + 1)
        m = jnp.maximum(jnp.max(s_l, axis=-1, keepdims=True), jnp.max(s_c, axis=-1, keepdims=True))
        e_l = jnp.exp2(s_l - m).astype(BF16)
        e_c = jnp.exp2(s_c - m).astype(BF16)
        ox = (jnp.dot(e_l, vl_ref[0, :, gx], preferred_element_type=F32)
              + jnp.dot(e_c, vc_ref[0, :, gx], preferred_element_type=F32))
        o = ox[:, :HD] / ox[:, HD:]
        o_ref[0, :, sl] = (o * _silu(z_ref[0, :, sl].astype(F32))).astype(BF16)


def _attn_call(p, kn, kn_ctx, vx, vx_ctx, q_norm_w, cosf, sinf):
    b = p.shape[0]
    qw = HEADS * HD
    kvw = KV_HEADS * HD
    return pl.pallas_call(
        _attn_kernel,
        grid=(b, SEQ // ATT_TQ),
        in_specs=[pl.BlockSpec((1, ATT_TQ, qw), lambda bi, i: (bi, i, COL_BQ // qw)),
                  pl.BlockSpec((1, ATT_TQ, qw), lambda bi, i: (bi, i, COL_BZ // qw)),
                  pl.BlockSpec((1, SEQ, kvw), lambda bi, i: (bi, 0, 0)),
                  pl.BlockSpec((1, CTX_LEN, kvw), lambda bi, i: (bi, 0, 0)),
                  pl.BlockSpec((1, SEQ, 2 * kvw), lambda bi, i: (bi, 0, 0)),
                  pl.BlockSpec((1, CTX_LEN, 2 * kvw), lambda bi, i: (bi, 0, 0)),
                  pl.BlockSpec((1, HD), lambda bi, i: (0, 0)),
                  pl.BlockSpec((ATT_TQ, HD), lambda bi, i: (i, 0)),
                  pl.BlockSpec((ATT_TQ, HD), lambda bi, i: (i, 0))],
        out_specs=pl.BlockSpec((1, ATT_TQ, qw), lambda bi, i: (bi, i, 0)),
        out_shape=jax.ShapeDtypeStruct((b, SEQ, qw), BF16),
        compiler_params=_params(2),
        name="attn",
    )(p, p, kn, kn_ctx, vx, vx_ctx, q_norm_w, cosf, sinf)


MERGE_T = 512


def _merge_kernel(of_ref, ob_ref, az_ref, bl_ref, bra_ref, brb_ref, x_ref, gate_ref, anw_ref,
                  wpa_ref, wpb_ref, wo_ref, out_ref, a_ref):
    o = of_ref[0].astype(F32) + ob_ref[0].astype(F32)
    for h in range(HEADS):
        sl = slice(h * HD, (h + 1) * HD)
        oh = o[:, sl]
        r = lax.rsqrt(jnp.mean(oh * oh, axis=-1, keepdims=True) + EPS)
        a_ref[:, sl] = (oh * r * anw_ref[...] * _silu(az_ref[0, :, sl].astype(F32))).astype(BF16)
    pa = jnp.dot(a_ref[...], wpa_ref[...], preferred_element_type=F32)
    pb = jnp.dot(bl_ref[0], wpb_ref[...], preferred_element_type=F32)
    m = (jax.nn.sigmoid(bra_ref[0].astype(F32)) * pa + jax.nn.sigmoid(brb_ref[0].astype(F32)) * pb)
    y = jnp.dot(m.astype(BF16), wo_ref[...], preferred_element_type=F32)
    out_ref[0] = x_ref[0] + gate_ref[0] * y


def _merge_call(o_f, o_b, p, b_lat, x, mod3, a_norm_w, wpa, wpb, wo):
    b, t, d = x.shape
    tm = MERGE_T
    tile = lambda bi, i: (bi, i, 0)
    const = lambda bi, i: (0, 0)
    return pl.pallas_call(
        _merge_kernel,
        grid=(b, t // tm),
        in_specs=[pl.BlockSpec((1, tm, 1024), tile),
                  pl.BlockSpec((1, tm, 1024), tile),
                  pl.BlockSpec((1, tm, 1024), lambda bi, i: (bi, i, COL_AZ // 1024)),
                  pl.BlockSpec((1, tm, 1024), tile),
                  pl.BlockSpec((1, tm, d), lambda bi, i: (bi, i, COL_BRA // d)),
                  pl.BlockSpec((1, tm, d), lambda bi, i: (bi, i, COL_BRB // d)),
                  pl.BlockSpec((1, tm, d), tile),
                  pl.BlockSpec((1, 1, d), lambda bi, i: (bi, 0, 2)),
                  pl.BlockSpec((1, HD), const),
                  pl.BlockSpec((1024, d), const, pipeline_mode=pl.Buffered(1)),
                  pl.BlockSpec((1024, d), const, pipeline_mode=pl.Buffered(1)),
                  pl.BlockSpec((d, d), const, pipeline_mode=pl.Buffered(1))],
        out_specs=pl.BlockSpec((1, tm, d), tile),
        out_shape=jax.ShapeDtypeStruct((b, t, d), F32),
        scratch_shapes=[pltpu.VMEM((tm, 1024), BF16)],
        compiler_params=_params(2),
        name="merge",
    )(o_f, o_b, p, b_lat, p, p, x, mod3, a_norm_w, wpa, wpb, wo)


def _rope_tables():
    t = jnp.arange(SEQ)
    row = (t // GRID_W).astype(F32)
    col = (t % GRID_W).astype(F32)
    n_freq = HD // 4
    inv = ROPE_THETA ** (-jnp.arange(n_freq, dtype=F32) / n_freq)
    ang = jnp.concatenate([row[:, None] * inv, col[:, None] * inv], axis=-1)
    cos, sin = jnp.cos(ang), jnp.sin(ang)
    return jnp.concatenate([cos, cos], axis=-1), jnp.concatenate([-sin, sin], axis=-1)


def _gate_row(a):
    return jnp.zeros((1, LANES), F32).at[0, 2 * HEADS:4 * HEADS].set(a.reshape(-1).astype(F32))


def _layer(x, ctx, c, c_ctx, w_mod, b_mod, norm_w, w_in_t, conv_w, a_log, dt_bias,
           a_norm_w, q_norm_w, k_norm_w, w_proj_a, w_proj_b, w_out):
    b = x.shape[0]
    d = D_MODEL
    w_all = _wprep_call(w_in_t)
    wpa = w_proj_a.astype(BF16)
    wpb = w_proj_b.astype(BF16)
    wo = w_out.astype(BF16)

    cc = jnp.concatenate([c, c_ctx[None, :], jnp.zeros((8 - b - 1, d), F32)], axis=0)
    mod3 = _mod_call(cc, w_mod, b_mod[None, :]).reshape(8, 1, 3 * d)

    nw = norm_w[None, :]
    p = _inproj_call(x, mod3, nw, w_all, mod_row=lambda bi: bi, w_col=lambda j: j,
                     n_col=P_COLS // INPROJ_TN, tn=INPROJ_TN, name="inproj_latent")
    small_blk = COL_SMALL // 1024
    p_ctx = _inproj_call(ctx.reshape(1, b * CTX_LEN, d), mod3, nw, w_all, mod_row=lambda bi: b,
                         w_col=lambda j: jnp.where(j < 3, j, small_blk), n_col=CTX_COLS // 1024,
                         tn=1024, name="inproj_ctx").reshape(b, CTX_LEN, CTX_COLS)

    cosf, sinf = _rope_tables()
    alog_row, dtb_row, knw = _gate_row(a_log), _gate_row(dt_bias), k_norm_w[None, :]
    g, bg, kn, vx = _prep_call(p, conv_w, alog_row, dtb_row, knw, cosf, sinf, latent=True)
    g_ctx, bg_ctx, kn_ctx, vx_ctx = _prep_call(p_ctx, conv_w, alog_row, dtb_row, knw, None, None,
                                               latent=False)
    states = _gdn_call(g_ctx, bg_ctx)
    o_f, o_b = _gdn_call(g, bg, states)
    b_lat = _attn_call(p, kn, kn_ctx, vx, vx_ctx, q_norm_w[None, :], cosf, sinf)
    return _merge_call(o_f, o_b, p, b_lat, x, mod3, a_norm_w[None, :], wpa, wpb, wo)


def kernel(x, c, ctx, c_ctx, w_mod, b_mod, norm_w, w_in, conv_w, a_log, dt_bias, a_norm_w,
           q_norm_w, k_norm_w, w_proj_a, w_proj_b, w_out):
    assert w_mod.shape[0] == 1
    w_in_t = jnp.swapaxes(w_in, 1, 2).reshape(w_in.shape[2], w_in.shape[1])
    return _layer(x, ctx, c, c_ctx, w_mod[0], b_mod[0], norm_w[0], w_in_t, conv_w[0], a_log[0],
                  dt_bias[0], a_norm_w[0], q_norm_w[0], k_norm_w[0], w_proj_a[0], w_proj_b[0],
                  w_out[0])
```

```python
import functools
import math

import jax
import jax.numpy as jnp
from jax import lax
from jax.experimental import pallas as pl
from jax.experimental.pallas import tpu as pltpu

F32 = jnp.float32
BF16 = jnp.bfloat16

D_MODEL = 2048
SEQ = 2048
CTX_LEN = 256
GRID_W = 64
EPS = 1e-6
HEADS = 8
HD = 128
KV_HEADS = 2
A_CONV = 5
ROPE_THETA = 10000.0
CHUNK = 64

COL_AQ, COL_AK, COL_AV, COL_AZ, COL_BQ, COL_BZ, COL_BRA, COL_BRB, COL_SMALL = (
    0, 1024, 2048, 3072, 4096, 5120, 6144, 8192, 10240)
P_COLS = 11264
SMALL_BK, SMALL_BV, SMALL_BA = 0, 256, 512
CTX_COLS = 4096
CTX_COL_SMALL = 3072

VMEM_LIMIT = 56 * 1024 * 1024
LANES = 128
INPROJ_TN = 1024


def _silu(x):
    return x * jax.nn.sigmoid(x)


def _softplus(x):
    return jnp.maximum(x, 0.0) + jnp.log1p(jnp.exp(-jnp.abs(x)))


def _bdot(a, b):
    return jnp.dot(a.astype(BF16), b.astype(BF16), preferred_element_type=F32)


def _bdot_nt(a, b):
    return lax.dot_general(a.astype(BF16), b.astype(BF16), (((1,), (1,)), ((), ())),
                           preferred_element_type=F32)


def _bdot_tn(a, b):
    return lax.dot_general(a.astype(BF16), b.astype(BF16), (((0,), (0,)), ((), ())),
                           preferred_element_type=F32)


def _params(n_axes):
    return pltpu.CompilerParams(dimension_semantics=("arbitrary",) * n_axes,
                                vmem_limit_bytes=VMEM_LIMIT)


WP_ROWS = 512
WP_SHIFT = 32
WP_NBLK = P_COLS // WP_ROWS


def _wprep_src(n):
    return jnp.where(n < 10, n, jnp.where(n < 20, n + 1, jnp.where(n == 20, 10, 8)))


def _wprep_kernel(main_ref, extra_ref, o_ref):
    n = pl.program_id(0)
    keep = WP_ROWS - WP_SHIFT

    @pl.when(n < 8)
    def _():
        o_ref[...] = main_ref[...].astype(BF16)

    @pl.when(jnp.logical_and(n >= 8, n < WP_NBLK - 1))
    def _():
        o_ref[:keep] = main_ref[WP_SHIFT:].astype(BF16)
        o_ref[keep:] = extra_ref[...].astype(BF16)

    @pl.when(n == WP_NBLK - 1)
    def _():
        o_ref[:WP_SHIFT] = main_ref[:WP_SHIFT].astype(BF16)
        o_ref[WP_SHIFT:] = jnp.zeros((keep, o_ref.shape[1]), BF16)


def _wprep_call(w_t):
    d = w_t.shape[1]
    ratio = WP_ROWS // WP_SHIFT
    return pl.pallas_call(
        _wprep_kernel,
        grid=(WP_NBLK,),
        in_specs=[pl.BlockSpec((WP_ROWS, d), lambda n: (_wprep_src(n), 0)),
                  pl.BlockSpec((WP_SHIFT, d), lambda n: (ratio * _wprep_src(n) + ratio, 0))],
        out_specs=pl.BlockSpec((WP_ROWS, d), lambda n: (n, 0)),
        out_shape=jax.ShapeDtypeStruct((P_COLS, d), BF16),
        compiler_params=_params(1),
        name="wprep",
    )(w_t, w_t)


def _mod_kernel(c_ref, w_ref, b_ref, o_ref):
    o_ref[...] = _bdot(_silu(c_ref[...]), w_ref[...]) + b_ref[...]


def _mod_call(cc, w_mod, b_mod):
    tn = 768
    n = w_mod.shape[1]
    return pl.pallas_call(
        _mod_kernel,
        grid=(n // tn,),
        in_specs=[pl.BlockSpec((8, D_MODEL), lambda j: (0, 0)),
                  pl.BlockSpec((D_MODEL, tn), lambda j: (0, j)),
                  pl.BlockSpec((1, tn), lambda j: (0, j))],
        out_specs=pl.BlockSpec((8, tn), lambda j: (0, j)),
        out_shape=jax.ShapeDtypeStruct((8, n), F32),
        compiler_params=_params(1),
        name="mod",
    )(cc, w_mod, b_mod)


def _inproj_kernel(x_ref, shift_ref, scale_ref, nw_ref, w_ref, o_ref, h_ref):
    @pl.when(pl.program_id(2) == 0)
    def _():
        x = x_ref[0]
        r = lax.rsqrt(jnp.mean(x * x, axis=-1, keepdims=True) + EPS)
        gain = nw_ref[...] * (1.0 + scale_ref[0])
        h_ref[...] = ((x * r) * gain + shift_ref[0]).astype(BF16)

    o_ref[0] = _bdot_nt(h_ref[...], w_ref[...]).astype(BF16)


def _inproj_call(x, mod3, norm_w, w_all, *, mod_row, w_col, n_col, tn, name):
    nb, t, d = x.shape
    tm = 1024
    return pl.pallas_call(
        _inproj_kernel,
        grid=(nb, t // tm, n_col),
        in_specs=[pl.BlockSpec((1, tm, d), lambda bi, i, j: (bi, i, 0)),
                  pl.BlockSpec((1, 1, d), lambda bi, i, j: (mod_row(bi), 0, 0)),
                  pl.BlockSpec((1, 1, d), lambda bi, i, j: (mod_row(bi), 0, 1)),
                  pl.BlockSpec((1, d), lambda bi, i, j: (0, 0)),
                  pl.BlockSpec((tn, d), lambda bi, i, j: (w_col(j), 0))],
        out_specs=pl.BlockSpec((1, tm, tn), lambda bi, i, j: (bi, i, j)),
        out_shape=jax.ShapeDtypeStruct((nb, t, n_col * tn), BF16),
        scratch_shapes=[pltpu.VMEM((tm, d), BF16)],
        compiler_params=_params(3),
        name=name,
    )(x, mod3, mod3, norm_w, w_all)


PREP_T = 256
HALO = 16


CONV_OFFS = (-2, -1, 1, 2)


def _prep_kernel(*refs, latent):
    if latent:
        (main_ref, prev_ref, next_ref, small_ref, shift_ref, convw_ref, alog_ref, dtb_ref, knw_ref,
         cos_ref, sin_ref, g_ref, bg_ref, kn_ref, vx_ref) = refs
        i = pl.program_id(1)
        has_prev = i > 0
        has_next = i < pl.num_programs(1) - 1
        row8 = lax.broadcasted_iota(jnp.int32, (8, 1024), 0)
    else:
        (main_ref, small_ref, shift_ref, convw_ref, alog_ref, dtb_ref, knw_ref,
         g_ref, bg_ref, kn_ref, vx_ref) = refs
    t = PREP_T
    for c in range(3):
        cs = slice(c * 1024, (c + 1) * 1024)
        w = [convw_ref[k:k + 1, cs] for k in range(A_CONV)]
        x = main_ref[0, :, cs]
        xw = jnp.concatenate([x * w[k].astype(BF16) for k in (0, 1, 3, 4)], axis=0)
        acc = x.astype(F32) * w[2] + jnp.dot(shift_ref[...], xw, preferred_element_type=F32)
        if latent:
            pv = jnp.where(has_prev, prev_ref[0, HALO - 8:, cs].astype(F32), 0.0)
            nx = jnp.where(has_next, next_ref[0, :8, cs].astype(F32), 0.0)
            top = (jnp.where(row8 < 2, pltpu.roll(pv, 2, 0), 0.0) * w[0]
                   + jnp.where(row8 < 1, pltpu.roll(pv, 1, 0), 0.0) * w[1])
            bot = (jnp.where(row8 >= 7, pltpu.roll(nx, 7, 0), 0.0) * w[3]
                   + jnp.where(row8 >= 6, pltpu.roll(nx, 6, 0), 0.0) * w[4])
            acc = jnp.concatenate([acc[:8] + top, acc[8:t - 8], acc[t - 8:] + bot], axis=0)
        act = _silu(acc)
        if c == 2:
            g_ref[0, :, cs] = act.astype(BF16)
        else:
            mul = HD ** -0.5 if c == 0 else 1.0
            for h in range(HEADS):
                a = act[:, h * HD:(h + 1) * HD]
                r = lax.rsqrt(jnp.sum(a * a, axis=-1, keepdims=True) + EPS)
                g_ref[0, :, c * 1024 + h * HD:c * 1024 + (h + 1) * HD] = (a * (r * mul)).astype(BF16)

    ba = small_ref[0, :, SMALL_BA:SMALL_BA + LANES].astype(F32)
    beta = jax.nn.sigmoid(ba)
    gdec = -jnp.exp(alog_ref[...]) * _softplus(ba + dtb_ref[...])
    lane = lax.broadcasted_iota(jnp.int32, ba.shape, 1)
    bg_ref[0] = jnp.where(lane < 2 * HEADS, beta, gdec)

    for h in range(KV_HEADS):
        kh = small_ref[0, :, SMALL_BK + h * HD:SMALL_BK + (h + 1) * HD].astype(F32)
        r = lax.rsqrt(jnp.mean(kh * kh, axis=-1, keepdims=True) + EPS)
        kn = kh * r * knw_ref[...]
        if latent:
            kn = kn * cos_ref[...] + pltpu.roll(kn, HD // 2, 1) * sin_ref[...]
        kn_ref[0, :, h * HD:(h + 1) * HD] = kn.astype(BF16)
        vx_ref[0, :, 2 * h * HD:(2 * h + 1) * HD] = small_ref[0, :, SMALL_BV + h * HD:SMALL_BV + (h + 1) * HD]
        vx_ref[0, :, (2 * h + 1) * HD:(2 * h + 2) * HD] = jnp.ones((PREP_T, HD), BF16)


def _prep_call(p, conv_w, alog_row, dtb_row, k_norm_w, cosf, sinf, *, latent):
    b, t, _ = p.shape
    nt = t // PREP_T
    const = lambda bi, i: (0, 0)
    tile = lambda bi, i: (bi, i, 0)
    if latent:
        hb = PREP_T // HALO
        last_hb = t // HALO - 1
        small_col = COL_SMALL // 1024
        in_specs = [pl.BlockSpec((1, PREP_T, 3072), tile),
                    pl.BlockSpec((1, HALO, 3072), lambda bi, i: (bi, jnp.maximum(i * hb - 1, 0), 0)),
                    pl.BlockSpec((1, HALO, 3072), lambda bi, i: (bi, jnp.minimum((i + 1) * hb, last_hb), 0)),
                    pl.BlockSpec((1, PREP_T, 1024), lambda bi, i: (bi, i, small_col))]
        args = [p, p, p, p]
    else:
        small_col = CTX_COL_SMALL // 1024
        in_specs = [pl.BlockSpec((1, PREP_T, 3072), tile),
                    pl.BlockSpec((1, PREP_T, 1024), lambda bi, i: (bi, i, small_col))]
        args = [p, p]
    shifts = jnp.concatenate([jnp.eye(PREP_T, k=off, dtype=BF16) for off in CONV_OFFS], axis=1)
    in_specs += [pl.BlockSpec(shifts.shape, const),
                 pl.BlockSpec((A_CONV, 3072), const), pl.BlockSpec((1, LANES), const),
                 pl.BlockSpec((1, LANES), const), pl.BlockSpec((1, HD), const)]
    args += [shifts, conv_w, alog_row, dtb_row, k_norm_w]
    if latent:
        in_specs += [pl.BlockSpec((PREP_T, HD), lambda bi, i: (i, 0))] * 2
        args += [cosf, sinf]
    return pl.pallas_call(
        functools.partial(_prep_kernel, latent=latent),
        grid=(b, nt),
        in_specs=in_specs,
        out_specs=[pl.BlockSpec((1, PREP_T, 3072), tile),
                   pl.BlockSpec((1, PREP_T, LANES), tile),
                   pl.BlockSpec((1, PREP_T, KV_HEADS * HD), tile),
                   pl.BlockSpec((1, PREP_T, 2 * KV_HEADS * HD), tile)],
        out_shape=[jax.ShapeDtypeStruct((b, t, 3072), BF16),
                   jax.ShapeDtypeStruct((b, t, LANES), F32),
                   jax.ShapeDtypeStruct((b, t, KV_HEADS * HD), BF16),
                   jax.ShapeDtypeStruct((b, t, 2 * KV_HEADS * HD), BF16)],
        compiler_params=_params(2),
        name="prep_latent" if latent else "prep_ctx",
    )(*args)


def _chunk_masks(direction):
    c = CHUNK
    ii = lax.broadcasted_iota(jnp.int32, (c, c), 0)
    jj = lax.broadcasted_iota(jnp.int32, (c, c), 1)
    if direction == 0:
        return ii >= jj, ii > jj, c - 1
    return ii <= jj, ii < jj, 0


def _chunk_cumsum(bg, incl):
    tri = incl.astype(BF16)
    g1 = bg.astype(BF16)
    r1 = bg - g1.astype(F32)
    g2 = r1.astype(BF16)
    g3 = (r1 - g2.astype(F32)).astype(BF16)
    return (jnp.dot(tri, g1, preferred_element_type=F32) + jnp.dot(tri, g2, preferred_element_type=F32)
            + jnp.dot(tri, g3, preferred_element_type=F32))


def _gdn_kernel(*refs, latent):
    step = pl.program_id(1)
    if latent:
        qkv_f_ref, bg_f_ref, qkv_b_ref, bg_b_ref, s0_f_ref, s0_b_ref, o_f_ref, o_b_ref, s_f_ref, s_b_ref = refs

        @pl.when(step == 0)
        def _():
            s_f_ref[...] = s0_f_ref[...]
            s_b_ref[...] = s0_b_ref[...]

        o_refs = (o_f_ref, o_b_ref)
    else:
        qkv_f_ref, bg_f_ref, qkv_b_ref, bg_b_ref, s_f_ref, s_b_ref = refs

        @pl.when(step == 0)
        def _():
            s_f_ref[...] = jnp.zeros(s_f_ref.shape, F32)
            s_b_ref[...] = jnp.zeros(s_b_ref.shape, F32)

    rows = list(range(qkv_f_ref.shape[0]))
    half = max(len(rows) // 2, 1)
    qkv_refs, bg_refs, s_refs = (qkv_f_ref, qkv_b_ref), (bg_f_ref, bg_b_ref), (s_f_ref, s_b_ref)
    for group in (rows[:half], rows[half:]):
        if group:
            st = _gdn_stage(qkv_refs, bg_refs, group, latent)
            _gdn_advance(st, s_refs, o_refs if latent else None, group, latent)


def _gdn_groups(rows):
    gp = GDN_PACK
    chains = [(n, d, h) for n in rows for d in range(2) for h in range(HEADS)]
    groups = [(n, d, hg) for n in rows for d in range(2) for hg in range(HEADS // gp)]
    members = {gr: [(gr[0], gr[1], gr[2] * gp + m) for m in range(gp)] for gr in groups}
    return chains, groups, members


def _gdn_stage(qkv_refs, bg_refs, rows, latent):
    c, gp = CHUNK, GDN_PACK
    chains, groups, members = _gdn_groups(rows)
    masks = [_chunk_masks(d) for d in range(2)]
    bgs = {(n, d): bg_refs[d][n] for n in rows for d in range(2)}
    gcs = {nd: _chunk_cumsum(bgs[nd], masks[nd[1]][0]) for nd in bgs}
    gcts = {nd: jnp.concatenate([gcs[nd], jnp.zeros((LANES - c, LANES), F32)], axis=0).T for nd in bgs}

    q, k, v, kf, beta, gcol, glast, decay = {}, {}, {}, {}, {}, {}, {}, {}
    for ch in chains:
        n, d, h = ch
        incl, _, last = masks[d]
        cb = d * HEADS + h
        cg = 2 * HEADS + cb
        q[ch] = qkv_refs[d][n, :, h * HD:(h + 1) * HD]
        k[ch] = qkv_refs[d][n, :, 1024 + h * HD:1024 + (h + 1) * HD]
        v[ch] = qkv_refs[d][n, :, 2048 + h * HD:2048 + (h + 1) * HD]
        kf[ch] = k[ch].astype(F32)
        beta[ch] = bgs[n, d][:, cb:cb + 1]
        gcol[ch] = gcs[n, d][:, cg:cg + 1]
        grow = gcts[n, d][cg:cg + 1, 0:c]
        glast[ch] = gcs[n, d][last:last + 1, cg:cg + 1]
        decay[ch] = jnp.where(incl, jnp.exp(jnp.where(incl, gcol[ch] - grow, 0.0)), 0.0)

    st = {}
    if latent:
        qk = {ch: _bdot_nt(q[ch], k[ch]) for ch in chains}
        st["attn"] = {ch: qk[ch] * decay[ch] for ch in chains}

    lane_k = lax.broadcasted_iota(jnp.int32, (c, gp * HD), 1)
    blk_k = [((lane_k >> (HD.bit_length() - 1)) == m).astype(BF16) for m in range(gp)]
    k4 = {(n, d, hg): qkv_refs[d][n, :, 1024 + hg * gp * HD:1024 + (hg + 1) * gp * HD]
          for n, d, hg in groups}
    kk4 = {gr: _bdot_nt(k4[gr], jnp.concatenate([k4[gr] * blk_k[m] for m in range(gp)], axis=0))
           for gr in groups}
    st["l4"] = {gr: kk4[gr] * jnp.concatenate(
        [beta[ch] * jnp.where(masks[ch[1]][1], decay[ch], 0.0) for ch in members[gr]], axis=1)
        for gr in groups}
    eg = {ch: jnp.exp(gcol[ch]) for ch in chains}
    rhs = {ch: jnp.concatenate([v[ch].astype(F32) * beta[ch], kf[ch] * (beta[ch] * eg[ch])], axis=1)
           for ch in chains}
    st["rhs4"] = {gr: jnp.concatenate([rhs[ch] for ch in members[gr]], axis=0).astype(BF16)
                  for gr in groups}
    if latent:
        st["q_dec"] = {ch: q[ch].astype(F32) * eg[ch] for ch in chains}
    st["k_dec"] = {ch: kf[ch] * jnp.exp(glast[ch] - gcol[ch]) for ch in chains}
    st["g_tot"] = {ch: jnp.exp(glast[ch]) for ch in chains}
    return st


def _gdn_advance(st, s_refs, o_refs, rows, latent):
    c, gp = CHUNK, GDN_PACK
    chains, groups, members = _gdn_groups(rows)
    state = lambda n, d, h: s_refs[d].at[n, h]
    l4 = st["l4"]

    lane_c = lax.broadcasted_iota(jnp.int32, (c, gp * c), 1)
    row_c = lax.broadcasted_iota(jnp.int32, (c, gp * c), 0)
    eye4 = ((lane_c & (c - 1)) == row_c).astype(F32)
    blk_c = [((lane_c >> (c.bit_length() - 1)) == m).astype(BF16) for m in range(gp)]

    def bd(x4):
        xb = x4.astype(BF16)
        return jnp.concatenate([xb * blk_c[m] for m in range(gp)], axis=0)

    def pdot(a, b):
        return jnp.dot(a.astype(BF16), b, preferred_element_type=F32)

    t4 = {gr: eye4 - l4[gr] for gr in groups}
    lp4 = {gr: pdot(l4[gr], bd(l4[gr])) for gr in groups}
    n_sq = c.bit_length() - 2
    for lvl in range(n_sq):
        last_lvl = lvl == n_sq - 1
        lhs4 = {gr: t4[gr] if last_lvl else jnp.concatenate([t4[gr], lp4[gr]], axis=0) for gr in groups}
        res = {gr: pdot(lhs4[gr], bd(lp4[gr])) for gr in groups}
        t4 = {gr: t4[gr] + res[gr][:c] for gr in groups}
        if not last_lvl:
            lp4 = {gr: res[gr][c:] for gr in groups}

    t0 = {gr: t4[gr].astype(BF16) for gr in groups}
    a4 = {gr: eye4 + l4[gr] for gr in groups}
    a_hi = {gr: a4[gr].astype(BF16) for gr in groups}
    a_lo = {gr: (a4[gr] - a_hi[gr].astype(F32)).astype(BF16) for gr in groups}
    bd_t0 = {gr: bd(t0[gr]) for gr in groups}
    at = {gr: pdot(jnp.concatenate([a_hi[gr], a_lo[gr]], axis=0), bd_t0[gr]) for gr in groups}
    resid = {gr: eye4 - (at[gr][:c] + at[gr][c:]) for gr in groups}
    t1 = {gr: t0[gr].astype(F32) + pdot(t0[gr], bd(resid[gr])) for gr in groups}

    uw4 = {gr: jnp.dot(bd(t1[gr]), st["rhs4"][gr], preferred_element_type=F32) for gr in groups}
    uw = {ch: uw4[gr][m * c:(m + 1) * c] for gr in groups for m, ch in enumerate(members[gr])}
    s = {ch: state(*ch)[...] for ch in chains}
    if latent:
        lhs = {ch: jnp.concatenate([uw[ch][:, HD:], st["q_dec"][ch]], axis=0) for ch in chains}
    else:
        lhs = {ch: uw[ch][:, HD:] for ch in chains}
    ws_qs = {ch: _bdot(lhs[ch], s[ch]) for ch in chains}
    v_new = {ch: uw[ch][:, :HD] - ws_qs[ch][:c] for ch in chains}
    kv = {ch: _bdot_tn(st["k_dec"][ch], v_new[ch]) for ch in chains}
    if latent:
        av = {ch: _bdot(st["attn"][ch], v_new[ch]) for ch in chains}
    for ch in chains:
        n, d, h = ch
        if latent:
            o_refs[d][n, :, h * HD:(h + 1) * HD] = (ws_qs[ch][c:] + av[ch]).astype(BF16)
        state(*ch)[...] = s[ch] * st["g_tot"][ch] + kv[ch]


GDN_ROWS = 4
GDN_PACK = 4


def _gdn_call(g, bg, states=None):
    b, t, _ = g.shape
    nc = t // CHUNK
    nr = GDN_ROWS
    latent = states is not None
    fwd = lambda bi, s: (bi, s, 0)
    bwd = lambda bi, s: (bi, nc - 1 - s, 0)
    in_specs = [pl.BlockSpec((nr, CHUNK, 3072), fwd), pl.BlockSpec((nr, CHUNK, LANES), fwd),
                pl.BlockSpec((nr, CHUNK, 3072), bwd), pl.BlockSpec((nr, CHUNK, LANES), bwd)]
    state_spec = pl.BlockSpec((nr, HEADS, HD, HD), lambda bi, s: (bi, 0, 0, 0))
    state_shape = jax.ShapeDtypeStruct((b, HEADS, HD, HD), F32)
    if latent:
        return pl.pallas_call(
            functools.partial(_gdn_kernel, latent=True),
            grid=(b // nr, nc),
            in_specs=in_specs + [state_spec, state_spec],
            out_specs=[pl.BlockSpec((nr, CHUNK, HEADS * HD), fwd),
                       pl.BlockSpec((nr, CHUNK, HEADS * HD), bwd)],
            out_shape=[jax.ShapeDtypeStruct((b, t, HEADS * HD), BF16)] * 2,
            scratch_shapes=[pltpu.VMEM((nr, HEADS, HD, HD), F32)] * 2,
            compiler_params=_params(2),
            name="gdn_latent",
        )(g, bg, g, bg, *states)
    return pl.pallas_call(
        functools.partial(_gdn_kernel, latent=False),
        grid=(b // nr, nc),
        in_specs=in_specs,
        out_specs=[state_spec, state_spec],
        out_shape=[state_shape, state_shape],
        compiler_params=_params(2),
        name="gdn_ctx",
    )(g, bg, g, bg)


ATT_TQ = 256
Q_PER_KV = HEADS // KV_HEADS
Q_SCALE = HD ** -0.5 * math.log2(math.e)


def _attn_kernel(q_ref, z_ref, kl_ref, kc_ref, vl_ref, vc_ref, qnw_ref, cos_ref, sin_ref, o_ref):
    def scores(j):
        g = slice((j // Q_PER_KV) * HD, (j // Q_PER_KV + 1) * HD)
        qh = q_ref[0, :, j * HD:(j + 1) * HD].astype(F32)
        r = lax.rsqrt(jnp.mean(qh * qh, axis=-1, keepdims=True) + EPS)
        qn = qh * r * qnw_ref[...]
        qr = (qn * cos_ref[...] + pltpu.roll(qn, HD // 2, 1) * sin_ref[...]) * Q_SCALE
        return _bdot_nt(qr, kl_ref[0, :, g]), _bdot_nt(qr, kc_ref[0, :, g])

    nxt = scores(0)
    for j in range(HEADS):
        sl = slice(j * HD, (j + 1) * HD)
        gx = slice((j // Q_PER_KV) * 2 * HD, (j // Q_PER_KV + 1) * 2 * HD)
        s_l, s_c = nxt
        if j + 1 < HEADS:
            nxt = scores(j + 1)
        m = jnp.maximum(jnp.max(s_l, axis=-1, keepdims=True), jnp.max(s_c, axis=-1, keepdims=True))
        e_l = jnp.exp2(s_l - m).astype(BF16)
        e_c = jnp.exp2(s_c - m).astype(BF16)
        ox = (jnp.dot(e_l, vl_ref[0, :, gx], preferred_element_type=F32)
              + jnp.dot(e_c, vc_ref[0, :, gx], preferred_element_type=F32))
        o = ox[:, :HD] / ox[:, HD:]
        o_ref[0, :, sl] = (o * _silu(z_ref[0, :, sl].astype(F32))).astype(BF16)


def _attn_call(p, kn, kn_ctx, vx, vx_ctx, q_norm_w, cosf, sinf):
    b = p.shape[0]
    qw = HEADS * HD
    kvw = KV_HEADS * HD
    return pl.pallas_call(
        _attn_kernel,
        grid=(b, SEQ // ATT_TQ),
        in_specs=[pl.BlockSpec((1, ATT_TQ, qw), lambda bi, i: (bi, i, COL_BQ // qw)),
                  pl.BlockSpec((1, ATT_TQ, qw), lambda bi, i: (bi, i, COL_BZ // qw)),
                  pl.BlockSpec((1, SEQ, kvw), lambda bi, i: (bi, 0, 0)),
                  pl.BlockSpec((1, CTX_LEN, kvw), lambda bi, i: (bi, 0, 0)),
                  pl.BlockSpec((1, SEQ, 2 * kvw), lambda bi, i: (bi, 0, 0)),
                  pl.BlockSpec((1, CTX_LEN, 2 * kvw), lambda bi, i: (bi, 0, 0)),
                  pl.BlockSpec((1, HD), lambda bi, i: (0, 0)),
                  pl.BlockSpec((ATT_TQ, HD), lambda bi, i: (i, 0)),
                  pl.BlockSpec((ATT_TQ, HD), lambda bi, i: (i, 0))],
        out_specs=pl.BlockSpec((1, ATT_TQ, qw), lambda bi, i: (bi, i, 0)),
        out_shape=jax.ShapeDtypeStruct((b, SEQ, qw), BF16),
        compiler_params=_params(2),
        name="attn",
    )(p, p, kn, kn_ctx, vx, vx_ctx, q_norm_w, cosf, sinf)


MERGE_T = 512


def _merge_kernel(of_ref, ob_ref, az_ref, bl_ref, bra_ref, brb_ref, x_ref, gate_ref, anw_ref,
                  wpa_ref, wpb_ref, wo_ref, out_ref, a_ref):
    o = of_ref[0].astype(F32) + ob_ref[0].astype(F32)
    for h in range(HEADS):
        sl = slice(h * HD, (h + 1) * HD)
        oh = o[:, sl]
        r = lax.rsqrt(jnp.mean(oh * oh, axis=-1, keepdims=True) + EPS)
        a_ref[:, sl] = (oh * r * anw_ref[...] * _silu(az_ref[0, :, sl].astype(F32))).astype(BF16)
    pa = jnp.dot(a_ref[...], wpa_ref[...], preferred_element_type=F32)
    pb = jnp.dot(bl_ref[0], wpb_ref[...], preferred_element_type=F32)
    m = (jax.nn.sigmoid(bra_ref[0].astype(F32)) * pa + jax.nn.sigmoid(brb_ref[0].astype(F32)) * pb)
    y = jnp.dot(m.astype(BF16), wo_ref[...], preferred_element_type=F32)
    out_ref[0] = x_ref[0] + gate_ref[0] * y


def _merge_call(o_f, o_b, p, b_lat, x, mod3, a_norm_w, wpa, wpb, wo):
    b, t, d = x.shape
    tm = MERGE_T
    tile = lambda bi, i: (bi, i, 0)
    const = lambda bi, i: (0, 0)
    return pl.pallas_call(
        _merge_kernel,
        grid=(b, t // tm),
        in_specs=[pl.BlockSpec((1, tm, 1024), tile),
                  pl.BlockSpec((1, tm, 1024), tile),
                  pl.BlockSpec((1, tm, 1024), lambda bi, i: (bi, i, COL_AZ // 1024)),
                  pl.BlockSpec((1, tm, 1024), tile),
                  pl.BlockSpec((1, tm, d), lambda bi, i: (bi, i, COL_BRA // d)),
                  pl.BlockSpec((1, tm, d), lambda bi, i: (bi, i, COL_BRB // d)),
                  pl.BlockSpec((1, tm, d), tile),
                  pl.BlockSpec((1, 1, d), lambda bi, i: (bi, 0, 2)),
                  pl.BlockSpec((1, HD), const),
                  pl.BlockSpec((1024, d), const, pipeline_mode=pl.Buffered(1)),
                  pl.BlockSpec((1024, d), const, pipeline_mode=pl.Buffered(1)),
                  pl.BlockSpec((d, d), const, pipeline_mode=pl.Buffered(1))],
        out_specs=pl.BlockSpec((1, tm, d), tile),
        out_shape=jax.ShapeDtypeStruct((b, t, d), F32),
        scratch_shapes=[pltpu.VMEM((tm, 1024), BF16)],
        compiler_params=_params(2),
        name="merge",
    )(o_f, o_b, p, b_lat, p, p, x, mod3, a_norm_w, wpa, wpb, wo)


def _rope_tables():
    t = jnp.arange(SEQ)
    row = (t // GRID_W).astype(F32)
    col = (t % GRID_W).astype(F32)
    n_freq = HD // 4
    inv = ROPE_THETA ** (-jnp.arange(n_freq, dtype=F32) / n_freq)
    ang = jnp.concatenate([row[:, None] * inv, col[:, None] * inv], axis=-1)
    cos, sin = jnp.cos(ang), jnp.sin(ang)
    return jnp.concatenate([cos, cos], axis=-1), jnp.concatenate([-sin, sin], axis=-1)


def _gate_row(a):
    return jnp.zeros((1, LANES), F32).at[0, 2 * HEADS:4 * HEADS].set(a.reshape(-1).astype(F32))


def _layer(x, ctx, c, c_ctx, w_mod, b_mod, norm_w, w_in_t, conv_w, a_log, dt_bias,
           a_norm_w, q_norm_w, k_norm_w, w_proj_a, w_proj_b, w_out):
    b = x.shape[0]
    d = D_MODEL
    w_all = _wprep_call(w_in_t)
    wpa = w_proj_a.astype(BF16)
    wpb = w_proj_b.astype(BF16)
    wo = w_out.astype(BF16)

    cc = jnp.concatenate([c, c_ctx[None, :], jnp.zeros((8 - b - 1, d), F32)], axis=0)
    mod3 = _mod_call(cc, w_mod, b_mod[None, :]).reshape(8, 1, 3 * d)

    nw = norm_w[None, :]
    p = _inproj_call(x, mod3, nw, w_all, mod_row=lambda bi: bi, w_col=lambda j: j,
                     n_col=P_COLS // INPROJ_TN, tn=INPROJ_TN, name="inproj_latent")
    small_blk = COL_SMALL // 1024
    p_ctx = _inproj_call(ctx.reshape(1, b * CTX_LEN, d), mod3, nw, w_all, mod_row=lambda bi: b,
                         w_col=lambda j: jnp.where(j < 3, j, small_blk), n_col=CTX_COLS // 1024,
                         tn=1024, name="inproj_ctx").reshape(b, CTX_LEN, CTX_COLS)

    cosf, sinf = _rope_tables()
    alog_row, dtb_row, knw = _gate_row(a_log), _gate_row(dt_bias), k_norm_w[None, :]
    g, bg, kn, vx = _prep_call(p, conv_w, alog_row, dtb_row, knw, cosf, sinf, latent=True)
    g_ctx, bg_ctx, kn_ctx, vx_ctx = _prep_call(p_ctx, conv_w, alog_row, dtb_row, knw, None, None,
                                               latent=False)
    states = _gdn_call(g_ctx, bg_ctx)
    o_f, o_b = _gdn_call(g, bg, states)
    b_lat = _attn_call(p, kn, kn_ctx, vx, vx_ctx, q_norm_w[None, :], cosf, sinf)
    return _merge_call(o_f, o_b, p, b_lat, x, mod3, a_norm_w[None, :], wpa, wpb, wo)


def kernel(x, c, ctx, c_ctx, w_mod, b_mod, norm_w, w_in, conv_w, a_log, dt_bias, a_norm_w,
           q_norm_w, k_norm_w, w_proj_a, w_proj_b, w_out):
    assert w_mod.shape[0] == 1
    w_in_t = jnp.swapaxes(w_in, 1, 2).reshape(w_in.shape[2], w_in.shape[1])
    return _layer(x, ctx, c, c_ctx, w_mod[0], b_mod[0], norm_w[0], w_in_t, conv_w[0], a_log[0],
                  dt_bias[0], a_norm_w[0], q_norm_w[0], k_norm_w[0], w_proj_a[0], w_proj_b[0],
                  w_out[0])
```

```python
import functools
import math

import jax
import jax.numpy as jnp
from jax import lax
from jax.experimental import pallas as pl
from jax.experimental.pallas import tpu as pltpu

F32 = jnp.float32
BF16 = jnp.bfloat16

D_MODEL = 2048
SEQ = 2048
CTX_LEN = 256
GRID_W = 64
EPS = 1e-6
HEADS = 8
HD = 128
KV_HEADS = 2
A_CONV = 5
ROPE_THETA = 10000.0
CHUNK = 64

COL_AQ, COL_AK, COL_AV, COL_AZ, COL_BQ, COL_BZ, COL_BRA, COL_BRB, COL_SMALL = (
    0, 1024, 2048, 3072, 4096, 5120, 6144, 8192, 10240)
P_COLS = 11264
SMALL_BK, SMALL_BV, SMALL_BA = 0, 256, 512
CTX_COLS = 4096
CTX_COL_SMALL = 3072

VMEM_LIMIT = 56 * 1024 * 1024
LANES = 128
INPROJ_VMEM_LIMIT = 62 * 1024 * 1024
INPROJ_TN = 2816


def _silu(x):
    return x * jax.nn.sigmoid(x)


def _softplus(x):
    return jnp.maximum(x, 0.0) + jnp.log1p(jnp.exp(-jnp.abs(x)))


def _bdot(a, b):
    return jnp.dot(a.astype(BF16), b.astype(BF16), preferred_element_type=F32)


def _bdot_nt(a, b):
    return lax.dot_general(a.astype(BF16), b.astype(BF16), (((1,), (1,)), ((), ())),
                           preferred_element_type=F32)


def _bdot_tn(a, b):
    return lax.dot_general(a.astype(BF16), b.astype(BF16), (((0,), (0,)), ((), ())),
                           preferred_element_type=F32)


def _params(n_axes, vmem_limit=VMEM_LIMIT):
    return pltpu.CompilerParams(dimension_semantics=("arbitrary",) * n_axes,
                                vmem_limit_bytes=vmem_limit)


WP_ROWS = 512
WP_SHIFT = 32
WP_NBLK = P_COLS // WP_ROWS


def _wprep_src(n):
    return jnp.where(n < 10, n, jnp.where(n < 20, n + 1, jnp.where(n == 20, 10, 8)))


def _wprep_kernel(main_ref, extra_ref, o_ref):
    n = pl.program_id(0)
    keep = WP_ROWS - WP_SHIFT

    @pl.when(n < 8)
    def _():
        o_ref[...] = main_ref[...].astype(BF16)

    @pl.when(jnp.logical_and(n >= 8, n < WP_NBLK - 1))
    def _():
        o_ref[:keep] = main_ref[WP_SHIFT:].astype(BF16)
        o_ref[keep:] = extra_ref[...].astype(BF16)

    @pl.when(n == WP_NBLK - 1)
    def _():
        o_ref[:WP_SHIFT] = main_ref[:WP_SHIFT].astype(BF16)
        o_ref[WP_SHIFT:] = jnp.zeros((keep, o_ref.shape[1]), BF16)


def _wprep_call(w_t):
    d = w_t.shape[1]
    ratio = WP_ROWS // WP_SHIFT
    return pl.pallas_call(
        _wprep_kernel,
        grid=(WP_NBLK,),
        in_specs=[pl.BlockSpec((WP_ROWS, d), lambda n: (_wprep_src(n), 0)),
                  pl.BlockSpec((WP_SHIFT, d), lambda n: (ratio * _wprep_src(n) + ratio, 0))],
        out_specs=pl.BlockSpec((WP_ROWS, d), lambda n: (n, 0)),
        out_shape=jax.ShapeDtypeStruct((P_COLS, d), BF16),
        compiler_params=_params(1),
        name="wprep",
    )(w_t, w_t)


def _mod_kernel(c_ref, w_ref, b_ref, o_ref):
    o_ref[...] = _bdot(_silu(c_ref[...]), w_ref[...]) + b_ref[...]


def _mod_call(cc, w_mod, b_mod):
    tn = 768
    n = w_mod.shape[1]
    return pl.pallas_call(
        _mod_kernel,
        grid=(n // tn,),
        in_specs=[pl.BlockSpec((8, D_MODEL), lambda j: (0, 0)),
                  pl.BlockSpec((D_MODEL, tn), lambda j: (0, j)),
                  pl.BlockSpec((1, tn), lambda j: (0, j))],
        out_specs=pl.BlockSpec((8, tn), lambda j: (0, j)),
        out_shape=jax.ShapeDtypeStruct((8, n), F32),
        compiler_params=_params(1),
        name="mod",
    )(cc, w_mod, b_mod)


def _inproj_kernel(x_ref, shift_ref, scale_ref, nw_ref, w_ref, o_ref, h_ref):
    @pl.when(pl.program_id(2) == 0)
    def _():
        x = x_ref[0]
        r = lax.rsqrt(jnp.mean(x * x, axis=-1, keepdims=True) + EPS)
        gain = nw_ref[...] * (1.0 + scale_ref[0])
        h_ref[...] = ((x * r) * gain + shift_ref[0]).astype(BF16)

    o_ref[0] = _bdot_nt(h_ref[...], w_ref[...]).astype(BF16)


def _inproj_call(x, mod3, norm_w, w_all, *, mod_row, w_col, n_col, tn, name):
    nb, t, d = x.shape
    tm = 1024
    return pl.pallas_call(
        _inproj_kernel,
        grid=(nb, t // tm, n_col),
        in_specs=[pl.BlockSpec((1, tm, d), lambda bi, i, j: (bi, i, 0)),
                  pl.BlockSpec((1, 1, d), lambda bi, i, j: (mod_row(bi), 0, 0)),
                  pl.BlockSpec((1, 1, d), lambda bi, i, j: (mod_row(bi), 0, 1)),
                  pl.BlockSpec((1, d), lambda bi, i, j: (0, 0)),
                  pl.BlockSpec((tn, d), lambda bi, i, j: (w_col(j), 0))],
        out_specs=pl.BlockSpec((1, tm, tn), lambda bi, i, j: (bi, i, j)),
        out_shape=jax.ShapeDtypeStruct((nb, t, n_col * tn), BF16),
        scratch_shapes=[pltpu.VMEM((tm, d), BF16)],
        compiler_params=_params(3, INPROJ_VMEM_LIMIT),
        name=name,
    )(x, mod3, mod3, norm_w, w_all)


PREP_T = 256
HALO = 16


CONV_OFFS = (-2, -1, 1, 2)


def _prep_kernel(*refs, latent):
    if latent:
        (main_ref, prev_ref, next_ref, small_ref, shift_ref, convw_ref, alog_ref, dtb_ref, knw_ref,
         cos_ref, sin_ref, g_ref, bg_ref, kn_ref, vx_ref) = refs
        i = pl.program_id(1)
        has_prev = i > 0
        has_next = i < pl.num_programs(1) - 1
        row8 = lax.broadcasted_iota(jnp.int32, (8, 1024), 0)
    else:
        (main_ref, small_ref, shift_ref, convw_ref, alog_ref, dtb_ref, knw_ref,
         g_ref, bg_ref, kn_ref, vx_ref) = refs
    t = PREP_T
    for c in range(3):
        cs = slice(c * 1024, (c + 1) * 1024)
        w = [convw_ref[k:k + 1, cs] for k in range(A_CONV)]
        x = main_ref[0, :, cs]
        xw = jnp.concatenate([x * w[k].astype(BF16) for k in (0, 1, 3, 4)], axis=0)
        acc = x.astype(F32) * w[2] + jnp.dot(shift_ref[...], xw, preferred_element_type=F32)
        if latent:
            pv = jnp.where(has_prev, prev_ref[0, HALO - 8:, cs].astype(F32), 0.0)
            nx = jnp.where(has_next, next_ref[0, :8, cs].astype(F32), 0.0)
            top = (jnp.where(row8 < 2, pltpu.roll(pv, 2, 0), 0.0) * w[0]
                   + jnp.where(row8 < 1, pltpu.roll(pv, 1, 0), 0.0) * w[1])
            bot = (jnp.where(row8 >= 7, pltpu.roll(nx, 7, 0), 0.0) * w[3]
                   + jnp.where(row8 >= 6, pltpu.roll(nx, 6, 0), 0.0) * w[4])
            acc = jnp.concatenate([acc[:8] + top, acc[8:t - 8], acc[t - 8:] + bot], axis=0)
        act = _silu(acc)
        if c == 2:
            g_ref[0, :, cs] = act.astype(BF16)
        else:
            mul = HD ** -0.5 if c == 0 else 1.0
            for h in range(HEADS):
                a = act[:, h * HD:(h + 1) * HD]
                r = lax.rsqrt(jnp.sum(a * a, axis=-1, keepdims=True) + EPS)
                g_ref[0, :, c * 1024 + h * HD:c * 1024 + (h + 1) * HD] = (a * (r * mul)).astype(BF16)

    ba = small_ref[0, :, SMALL_BA:SMALL_BA + LANES].astype(F32)
    beta = jax.nn.sigmoid(ba)
    gdec = -jnp.exp(alog_ref[...]) * _softplus(ba + dtb_ref[...])
    lane = lax.broadcasted_iota(jnp.int32, ba.shape, 1)
    bg_ref[0] = jnp.where(lane < 2 * HEADS, beta, gdec)

    for h in range(KV_HEADS):
        kh = small_ref[0, :, SMALL_BK + h * HD:SMALL_BK + (h + 1) * HD].astype(F32)
        r = lax.rsqrt(jnp.mean(kh * kh, axis=-1, keepdims=True) + EPS)
        kn = kh * r * knw_ref[...]
        if latent:
            kn = kn * cos_ref[...] + pltpu.roll(kn, HD // 2, 1) * sin_ref[...]
        kn_ref[0, :, h * HD:(h + 1) * HD] = kn.astype(BF16)
        vx_ref[0, :, 2 * h * HD:(2 * h + 1) * HD] = small_ref[0, :, SMALL_BV + h * HD:SMALL_BV + (h + 1) * HD]
        vx_ref[0, :, (2 * h + 1) * HD:(2 * h + 2) * HD] = jnp.ones((PREP_T, HD), BF16)


def _prep_call(p, conv_w, alog_row, dtb_row, k_norm_w, cosf, sinf, *, latent):
    b, t, _ = p.shape
    nt = t // PREP_T
    const = lambda bi, i: (0, 0)
    tile = lambda bi, i: (bi, i, 0)
    if latent:
        hb = PREP_T // HALO
        last_hb = t // HALO - 1
        small_col = COL_SMALL // 1024
        in_specs = [pl.BlockSpec((1, PREP_T, 3072), tile),
                    pl.BlockSpec((1, HALO, 3072), lambda bi, i: (bi, jnp.maximum(i * hb - 1, 0), 0)),
                    pl.BlockSpec((1, HALO, 3072), lambda bi, i: (bi, jnp.minimum((i + 1) * hb, last_hb), 0)),
                    pl.BlockSpec((1, PREP_T, 1024), lambda bi, i: (bi, i, small_col))]
        args = [p, p, p, p]
    else:
        small_col = CTX_COL_SMALL // 1024
        in_specs = [pl.BlockSpec((1, PREP_T, 3072), tile),
                    pl.BlockSpec((1, PREP_T, 1024), lambda bi, i: (bi, i, small_col))]
        args = [p, p]
    shifts = jnp.concatenate([jnp.eye(PREP_T, k=off, dtype=BF16) for off in CONV_OFFS], axis=1)
    in_specs += [pl.BlockSpec(shifts.shape, const),
                 pl.BlockSpec((A_CONV, 3072), const), pl.BlockSpec((1, LANES), const),
                 pl.BlockSpec((1, LANES), const), pl.BlockSpec((1, HD), const)]
    args += [shifts, conv_w, alog_row, dtb_row, k_norm_w]
    if latent:
        in_specs += [pl.BlockSpec((PREP_T, HD), lambda bi, i: (i, 0))] * 2
        args += [cosf, sinf]
    return pl.pallas_call(
        functools.partial(_prep_kernel, latent=latent),
        grid=(b, nt),
        in_specs=in_specs,
        out_specs=[pl.BlockSpec((1, PREP_T, 3072), tile),
                   pl.BlockSpec((1, PREP_T, LANES), tile),
                   pl.BlockSpec((1, PREP_T, KV_HEADS * HD), tile),
                   pl.BlockSpec((1, PREP_T, 2 * KV_HEADS * HD), tile)],
        out_shape=[jax.ShapeDtypeStruct((b, t, 3072), BF16),
                   jax.ShapeDtypeStruct((b, t, LANES), F32),
                   jax.ShapeDtypeStruct((b, t, KV_HEADS * HD), BF16),
                   jax.ShapeDtypeStruct((b, t, 2 * KV_HEADS * HD), BF16)],
        compiler_params=_params(2),
        name="prep_latent" if latent else "prep_ctx",
    )(*args)


def _chunk_masks(direction):
    c = CHUNK
    ii = lax.broadcasted_iota(jnp.int32, (c, c), 0)
    jj = lax.broadcasted_iota(jnp.int32, (c, c), 1)
    if direction == 0:
        return ii >= jj, ii > jj, c - 1
    return ii <= jj, ii < jj, 0


def _chunk_cumsum(bg, incl):
    tri = incl.astype(BF16)
    g1 = bg.astype(BF16)
    r1 = bg - g1.astype(F32)
    g2 = r1.astype(BF16)
    g3 = (r1 - g2.astype(F32)).astype(BF16)
    return (jnp.dot(tri, g1, preferred_element_type=F32) + jnp.dot(tri, g2, preferred_element_type=F32)
            + jnp.dot(tri, g3, preferred_element_type=F32))


def _gdn_kernel(*refs, latent):
    step = pl.program_id(1)
    if latent:
        qkv_f_ref, bg_f_ref, qkv_b_ref, bg_b_ref, s0_f_ref, s0_b_ref, o_f_ref, o_b_ref, s_f_ref, s_b_ref = refs

        @pl.when(step == 0)
        def _():
            s_f_ref[...] = s0_f_ref[...]
            s_b_ref[...] = s0_b_ref[...]

        o_refs = (o_f_ref, o_b_ref)
    else:
        qkv_f_ref, bg_f_ref, qkv_b_ref, bg_b_ref, s_f_ref, s_b_ref = refs

        @pl.when(step == 0)
        def _():
            s_f_ref[...] = jnp.zeros(s_f_ref.shape, F32)
            s_b_ref[...] = jnp.zeros(s_b_ref.shape, F32)

    state = lambda n, d, h: (s_f_ref, s_b_ref)[d].at[n, h]
    c = CHUNK
    qkv_refs = (qkv_f_ref, qkv_b_ref)
    rows = range(qkv_f_ref.shape[0])
    chains = [(n, d, h) for n in rows for d in range(2) for h in range(HEADS)]

    masks = [_chunk_masks(d) for d in range(2)]
    bg_refs = (bg_f_ref, bg_b_ref)
    bgs = {(n, d): bg_refs[d][n] for n in rows for d in range(2)}
    gcs = {nd: _chunk_cumsum(bgs[nd], masks[nd[1]][0]) for nd in bgs}
    gcts = {nd: jnp.concatenate([gcs[nd], jnp.zeros((LANES - c, LANES), F32)], axis=0).T for nd in bgs}

    q, k, v, kf, beta, gcol, glast, decay = {}, {}, {}, {}, {}, {}, {}, {}
    for ch in chains:
        n, d, h = ch
        incl, _, last = masks[d]
        cb = d * HEADS + h
        cg = 2 * HEADS + cb
        q[ch] = qkv_refs[d][n, :, h * HD:(h + 1) * HD]
        k[ch] = qkv_refs[d][n, :, 1024 + h * HD:1024 + (h + 1) * HD]
        v[ch] = qkv_refs[d][n, :, 2048 + h * HD:2048 + (h + 1) * HD]
        kf[ch] = k[ch].astype(F32)
        beta[ch] = bgs[n, d][:, cb:cb + 1]
        gcol[ch] = gcs[n, d][:, cg:cg + 1]
        grow = gcts[n, d][cg:cg + 1, 0:c]
        glast[ch] = gcs[n, d][last:last + 1, cg:cg + 1]
        decay[ch] = jnp.where(incl, jnp.exp(jnp.where(incl, gcol[ch] - grow, 0.0)), 0.0)

    if latent:
        qk = {ch: _bdot_nt(q[ch], k[ch]) for ch in chains}
        attn = {ch: qk[ch] * decay[ch] for ch in chains}

    gp = GDN_PACK
    groups = [(n, d, hg) for n in rows for d in range(2) for hg in range(HEADS // gp)]
    members = {gr: [(gr[0], gr[1], gr[2] * gp + m) for m in range(gp)] for gr in groups}
    lane_c = lax.broadcasted_iota(jnp.int32, (c, gp * c), 1)
    row_c = lax.broadcasted_iota(jnp.int32, (c, gp * c), 0)
    eye4 = ((lane_c & (c - 1)) == row_c).astype(F32)
    blk_c = [((lane_c >> (c.bit_length() - 1)) == m).astype(BF16) for m in range(gp)]
    lane_k = lax.broadcasted_iota(jnp.int32, (c, gp * HD), 1)
    blk_k = [((lane_k >> (HD.bit_length() - 1)) == m).astype(BF16) for m in range(gp)]

    def bd(x4):
        xb = x4.astype(BF16)
        return jnp.concatenate([xb * blk_c[m] for m in range(gp)], axis=0)

    def pdot(a, b):
        return jnp.dot(a.astype(BF16), b, preferred_element_type=F32)

    k4 = {(n, d, hg): qkv_refs[d][n, :, 1024 + hg * gp * HD:1024 + (hg + 1) * gp * HD]
          for n, d, hg in groups}
    kk4 = {gr: _bdot_nt(k4[gr], jnp.concatenate([k4[gr] * blk_k[m] for m in range(gp)], axis=0))
           for gr in groups}
    l4 = {gr: kk4[gr] * jnp.concatenate(
        [beta[ch] * jnp.where(masks[ch[1]][1], decay[ch], 0.0) for ch in members[gr]], axis=1)
        for gr in groups}

    t4 = {gr: eye4 - l4[gr] for gr in groups}
    lp4 = {gr: pdot(l4[gr], bd(l4[gr])) for gr in groups}
    n_sq = c.bit_length() - 2
    for lvl in range(n_sq):
        last_lvl = lvl == n_sq - 1
        lhs4 = {gr: t4[gr] if last_lvl else jnp.concatenate([t4[gr], lp4[gr]], axis=0) for gr in groups}
        res = {gr: pdot(lhs4[gr], bd(lp4[gr])) for gr in groups}
        t4 = {gr: t4[gr] + res[gr][:c] for gr in groups}
        if not last_lvl:
            lp4 = {gr: res[gr][c:] for gr in groups}

    t0 = {gr: t4[gr].astype(BF16) for gr in groups}
    a4 = {gr: eye4 + l4[gr] for gr in groups}
    a_hi = {gr: a4[gr].astype(BF16) for gr in groups}
    a_lo = {gr: (a4[gr] - a_hi[gr].astype(F32)).astype(BF16) for gr in groups}
    bd_t0 = {gr: bd(t0[gr]) for gr in groups}
    at = {gr: pdot(jnp.concatenate([a_hi[gr], a_lo[gr]], axis=0), bd_t0[gr]) for gr in groups}
    resid = {gr: eye4 - (at[gr][:c] + at[gr][c:]) for gr in groups}
    t1 = {gr: t0[gr].astype(F32) + pdot(t0[gr], bd(resid[gr])) for gr in groups}

    eg = {ch: jnp.exp(gcol[ch]) for ch in chains}
    rhs = {ch: jnp.concatenate([v[ch].astype(F32) * beta[ch], kf[ch] * (beta[ch] * eg[ch])], axis=1)
           for ch in chains}
    rhs4 = {gr: jnp.concatenate([rhs[ch] for ch in members[gr]], axis=0).astype(BF16) for gr in groups}
    uw4 = {gr: jnp.dot(bd(t1[gr]), rhs4[gr], preferred_element_type=F32) for gr in groups}
    uw = {ch: uw4[gr][m * c:(m + 1) * c] for gr in groups for m, ch in enumerate(members[gr])}
    s = {ch: state(*ch)[...] for ch in chains}
    if latent:
        lhs = {ch: jnp.concatenate([uw[ch][:, HD:], q[ch].astype(F32) * eg[ch]], axis=0) for ch in chains}
    else:
        lhs = {ch: uw[ch][:, HD:] for ch in chains}
    ws_qs = {ch: _bdot(lhs[ch], s[ch]) for ch in chains}
    v_new = {ch: uw[ch][:, :HD] - ws_qs[ch][:c] for ch in chains}
    k_dec = {ch: kf[ch] * jnp.exp(glast[ch] - gcol[ch]) for ch in chains}
    kv = {ch: _bdot_tn(k_dec[ch], v_new[ch]) for ch in chains}
    if latent:
        av = {ch: _bdot(attn[ch], v_new[ch]) for ch in chains}
    for ch in chains:
        n, d, h = ch
        if latent:
            o_refs[d][n, :, h * HD:(h + 1) * HD] = (ws_qs[ch][c:] + av[ch]).astype(BF16)
        state(*ch)[...] = s[ch] * jnp.exp(glast[ch]) + kv[ch]


GDN_ROWS = 4
GDN_PACK = 4


def _gdn_call(g, bg, states=None):
    b, t, _ = g.shape
    nc = t // CHUNK
    nr = GDN_ROWS
    latent = states is not None
    fwd = lambda bi, s: (bi, s, 0)
    bwd = lambda bi, s: (bi, nc - 1 - s, 0)
    in_specs = [pl.BlockSpec((nr, CHUNK, 3072), fwd), pl.BlockSpec((nr, CHUNK, LANES), fwd),
                pl.BlockSpec((nr, CHUNK, 3072), bwd), pl.BlockSpec((nr, CHUNK, LANES), bwd)]
    state_spec = pl.BlockSpec((nr, HEADS, HD, HD), lambda bi, s: (bi, 0, 0, 0))
    state_shape = jax.ShapeDtypeStruct((b, HEADS, HD, HD), F32)
    if latent:
        return pl.pallas_call(
            functools.partial(_gdn_kernel, latent=True),
            grid=(b // nr, nc),
            in_specs=in_specs + [state_spec, state_spec],
            out_specs=[pl.BlockSpec((nr, CHUNK, HEADS * HD), fwd),
                       pl.BlockSpec((nr, CHUNK, HEADS * HD), bwd)],
            out_shape=[jax.ShapeDtypeStruct((b, t, HEADS * HD), BF16)] * 2,
            scratch_shapes=[pltpu.VMEM((nr, HEADS, HD, HD), F32)] * 2,
            compiler_params=_params(2),
            name="gdn_latent",
        )(g, bg, g, bg, *states)
    return pl.pallas_call(
        functools.partial(_gdn_kernel, latent=False),
        grid=(b // nr, nc),
        in_specs=in_specs,
        out_specs=[state_spec, state_spec],
        out_shape=[state_shape, state_shape],
        compiler_params=_params(2),
        name="gdn_ctx",
    )(g, bg, g, bg)


ATT_TQ = 256
Q_PER_KV = HEADS // KV_HEADS
Q_SCALE = HD ** -0.5 * math.log2(math.e)


def _attn_kernel(q_ref, z_ref, kl_ref, kc_ref, vl_ref, vc_ref, qnw_ref, cos_ref, sin_ref, o_ref):
    def scores(j):
        g = slice((j // Q_PER_KV) * HD, (j // Q_PER_KV + 1) * HD)
        qh = q_ref[0, :, j * HD:(j + 1) * HD].astype(F32)
        r = lax.rsqrt(jnp.mean(qh * qh, axis=-1, keepdims=True) + EPS)
        qn = qh * r * qnw_ref[...]
        qr = (qn * cos_ref[...] + pltpu.roll(qn, HD // 2, 1) * sin_ref[...]) * Q_SCALE
        return _bdot_nt(qr, kl_ref[0, :, g]), _bdot_nt(qr, kc_ref[0, :, g])

    nxt = scores(0)
    for j in range(HEADS):
        sl = slice(j * HD, (j + 1) * HD)
        gx = slice((j // Q_PER_KV) * 2 * HD, (j // Q_PER_KV + 1) * 2 * HD)
        s_l, s_c = nxt
        if j + 1 < HEADS:
            nxt = scores(j + 1)
        m = jnp.maximum(jnp.max(s_l, axis=-1, keepdims=True), jnp.max(s_c, axis=-1, keepdims=True))
        e_l = jnp.exp2(s_l - m).astype(BF16)
        e_c = jnp.exp2(s_c - m).astype(BF16)
        ox = (jnp.dot(e_l, vl_ref[0, :, gx], preferred_element_type=F32)
              + jnp.dot(e_c, vc_ref[0, :, gx], preferred_element_type=F32))
        o = ox[:, :HD] / ox[:, HD:]
        o_ref[0, :, sl] = (o * _silu(z_ref[0, :, sl].astype(F32))).astype(BF16)


def _attn_call(p, kn, kn_ctx, vx, vx_ctx, q_norm_w, cosf, sinf):
    b = p.shape[0]
    qw = HEADS * HD
    kvw = KV_HEADS * HD
    return pl.pallas_call(
        _attn_kernel,
        grid=(b, SEQ // ATT_TQ),
        in_specs=[pl.BlockSpec((1, ATT_TQ, qw), lambda bi, i: (bi, i, COL_BQ // qw)),
                  pl.BlockSpec((1, ATT_TQ, qw), lambda bi, i: (bi, i, COL_BZ // qw)),
                  pl.BlockSpec((1, SEQ, kvw), lambda bi, i: (bi, 0, 0)),
                  pl.BlockSpec((1, CTX_LEN, kvw), lambda bi, i: (bi, 0, 0)),
                  pl.BlockSpec((1, SEQ, 2 * kvw), lambda bi, i: (bi, 0, 0)),
                  pl.BlockSpec((1, CTX_LEN, 2 * kvw), lambda bi, i: (bi, 0, 0)),
                  pl.BlockSpec((1, HD), lambda bi, i: (0, 0)),
                  pl.BlockSpec((ATT_TQ, HD), lambda bi, i: (i, 0)),
                  pl.BlockSpec((ATT_TQ, HD), lambda bi, i: (i, 0))],
        out_specs=pl.BlockSpec((1, ATT_TQ, qw), lambda bi, i: (bi, i, 0)),
        out_shape=jax.ShapeDtypeStruct((b, SEQ, qw), BF16),
        compiler_params=_params(2),
        name="attn",
    )(p, p, kn, kn_ctx, vx, vx_ctx, q_norm_w, cosf, sinf)


MERGE_T = 512


def _merge_kernel(of_ref, ob_ref, az_ref, bl_ref, bra_ref, brb_ref, x_ref, gate_ref, anw_ref,
                  wpa_ref, wpb_ref, wo_ref, out_ref, a_ref):
    o = of_ref[0].astype(F32) + ob_ref[0].astype(F32)
    for h in range(HEADS):
        sl = slice(h * HD, (h + 1) * HD)
        oh = o[:, sl]
        r = lax.rsqrt(jnp.mean(oh * oh, axis=-1, keepdims=True) + EPS)
        a_ref[:, sl] = (oh * r * anw_ref[...] * _silu(az_ref[0, :, sl].astype(F32))).astype(BF16)
    pa = jnp.dot(a_ref[...], wpa_ref[...], preferred_element_type=F32)
    pb = jnp.dot(bl_ref[0], wpb_ref[...], preferred_element_type=F32)
    m = (jax.nn.sigmoid(bra_ref[0].astype(F32)) * pa + jax.nn.sigmoid(brb_ref[0].astype(F32)) * pb)
    y = jnp.dot(m.astype(BF16), wo_ref[...], preferred_element_type=F32)
    out_ref[0] = x_ref[0] + gate_ref[0] * y


def _merge_call(o_f, o_b, p, b_lat, x, mod3, a_norm_w, wpa, wpb, wo):
    b, t, d = x.shape
    tm = MERGE_T
    tile = lambda bi, i: (bi, i, 0)
    const = lambda bi, i: (0, 0)
    return pl.pallas_call(
        _merge_kernel,
        grid=(b, t // tm),
        in_specs=[pl.BlockSpec((1, tm, 1024), tile),
                  pl.BlockSpec((1, tm, 1024), tile),
                  pl.BlockSpec((1, tm, 1024), lambda bi, i: (bi, i, COL_AZ // 1024)),
                  pl.BlockSpec((1, tm, 1024), tile),
                  pl.BlockSpec((1, tm, d), lambda bi, i: (bi, i, COL_BRA // d)),
                  pl.BlockSpec((1, tm, d), lambda bi, i: (bi, i, COL_BRB // d)),
                  pl.BlockSpec((1, tm, d), tile),
                  pl.BlockSpec((1, 1, d), lambda bi, i: (bi, 0, 2)),
                  pl.BlockSpec((1, HD), const),
                  pl.BlockSpec((1024, d), const, pipeline_mode=pl.Buffered(1)),
                  pl.BlockSpec((1024, d), const, pipeline_mode=pl.Buffered(1)),
                  pl.BlockSpec((d, d), const, pipeline_mode=pl.Buffered(1))],
        out_specs=pl.BlockSpec((1, tm, d), tile),
        out_shape=jax.ShapeDtypeStruct((b, t, d), F32),
        scratch_shapes=[pltpu.VMEM((tm, 1024), BF16)],
        compiler_params=_params(2),
        name="merge",
    )(o_f, o_b, p, b_lat, p, p, x, mod3, a_norm_w, wpa, wpb, wo)


def _rope_tables():
    t = jnp.arange(SEQ)
    row = (t // GRID_W).astype(F32)
    col = (t % GRID_W).astype(F32)
    n_freq = HD // 4
    inv = ROPE_THETA ** (-jnp.arange(n_freq, dtype=F32) / n_freq)
    ang = jnp.concatenate([row[:, None] * inv, col[:, None] * inv], axis=-1)
    cos, sin = jnp.cos(ang), jnp.sin(ang)
    return jnp.concatenate([cos, cos], axis=-1), jnp.concatenate([-sin, sin], axis=-1)


def _gate_row(a):
    return jnp.zeros((1, LANES), F32).at[0, 2 * HEADS:4 * HEADS].set(a.reshape(-1).astype(F32))


def _layer(x, ctx, c, c_ctx, w_mod, b_mod, norm_w, w_in_t, conv_w, a_log, dt_bias,
           a_norm_w, q_norm_w, k_norm_w, w_proj_a, w_proj_b, w_out):
    b = x.shape[0]
    d = D_MODEL
    w_all = _wprep_call(w_in_t)
    wpa = w_proj_a.astype(BF16)
    wpb = w_proj_b.astype(BF16)
    wo = w_out.astype(BF16)

    cc = jnp.concatenate([c, c_ctx[None, :], jnp.zeros((8 - b - 1, d), F32)], axis=0)
    mod3 = _mod_call(cc, w_mod, b_mod[None, :]).reshape(8, 1, 3 * d)

    nw = norm_w[None, :]
    p = _inproj_call(x, mod3, nw, w_all, mod_row=lambda bi: bi, w_col=lambda j: j,
                     n_col=P_COLS // INPROJ_TN, tn=INPROJ_TN, name="inproj_latent")
    small_blk = COL_SMALL // 1024
    p_ctx = _inproj_call(ctx.reshape(1, b * CTX_LEN, d), mod3, nw, w_all, mod_row=lambda bi: b,
                         w_col=lambda j: jnp.where(j < 3, j, small_blk), n_col=CTX_COLS // 1024,
                         tn=1024, name="inproj_ctx").reshape(b, CTX_LEN, CTX_COLS)

    cosf, sinf = _rope_tables()
    alog_row, dtb_row, knw = _gate_row(a_log), _gate_row(dt_bias), k_norm_w[None, :]
    g, bg, kn, vx = _prep_call(p, conv_w, alog_row, dtb_row, knw, cosf, sinf, latent=True)
    g_ctx, bg_ctx, kn_ctx, vx_ctx = _prep_call(p_ctx, conv_w, alog_row, dtb_row, knw, None, None,
                                               latent=False)
    states = _gdn_call(g_ctx, bg_ctx)
    o_f, o_b = _gdn_call(g, bg, states)
    b_lat = _attn_call(p, kn, kn_ctx, vx, vx_ctx, q_norm_w[None, :], cosf, sinf)
    return _merge_call(o_f, o_b, p, b_lat, x, mod3, a_norm_w[None, :], wpa, wpb, wo)


def kernel(x, c, ctx, c_ctx, w_mod, b_mod, norm_w, w_in, conv_w, a_log, dt_bias, a_norm_w,
           q_norm_w, k_norm_w, w_proj_a, w_proj_b, w_out):
    assert w_mod.shape[0] == 1
    w_in_t = jnp.swapaxes(w_in, 1, 2).reshape(w_in.shape[2], w_in.shape[1])
    return _layer(x, ctx, c, c_ctx, w_mod[0], b_mod[0], norm_w[0], w_in_t, conv_w[0], a_log[0],
                  dt_bias[0], a_norm_w[0], q_norm_w[0], k_norm_w[0], w_proj_a[0], w_proj_b[0],
                  w_out[0])
```

```python
import functools
import math

import jax
import jax.numpy as jnp
from jax import lax
from jax.experimental import pallas as pl
from jax.experimental.pallas import tpu as pltpu

F32 = jnp.float32
BF16 = jnp.bfloat16

D_MODEL = 2048
SEQ = 2048
CTX_LEN = 256
GRID_W = 64
EPS = 1e-6
HEADS = 8
HD = 128
KV_HEADS = 2
A_CONV = 5
ROPE_THETA = 10000.0
CHUNK = 64

COL_AQ, COL_AK, COL_AV, COL_AZ, COL_BQ, COL_BZ, COL_BRA, COL_BRB, COL_SMALL = (
    0, 1024, 2048, 3072, 4096, 5120, 6144, 8192, 10240)
P_COLS = 11264
SMALL_BK, SMALL_BV, SMALL_BA = 0, 256, 512
CTX_COLS = 4096
CTX_COL_SMALL = 3072

VMEM_LIMIT = 56 * 1024 * 1024
LANES = 128
INPROJ_VMEM_LIMIT = 62 * 1024 * 1024
INPROJ_TN = 2816


def _silu(x):
    return x * jax.nn.sigmoid(x)


def _softplus(x):
    return jnp.maximum(x, 0.0) + jnp.log1p(jnp.exp(-jnp.abs(x)))


def _bdot(a, b):
    return jnp.dot(a.astype(BF16), b.astype(BF16), preferred_element_type=F32)


def _bdot_nt(a, b):
    return lax.dot_general(a.astype(BF16), b.astype(BF16), (((1,), (1,)), ((), ())),
                           preferred_element_type=F32)


def _bdot_tn(a, b):
    return lax.dot_general(a.astype(BF16), b.astype(BF16), (((0,), (0,)), ((), ())),
                           preferred_element_type=F32)


def _params(n_axes, vmem_limit=VMEM_LIMIT):
    return pltpu.CompilerParams(dimension_semantics=("arbitrary",) * n_axes,
                                vmem_limit_bytes=vmem_limit)


WP_ROWS = 512
WP_SHIFT = 32
WP_NBLK = P_COLS // WP_ROWS


def _wprep_src(n):
    return jnp.where(n < 10, n, jnp.where(n < 20, n + 1, jnp.where(n == 20, 10, 8)))


def _wprep_kernel(main_ref, extra_ref, o_ref):
    n = pl.program_id(0)
    keep = WP_ROWS - WP_SHIFT

    @pl.when(n < 8)
    def _():
        o_ref[...] = main_ref[...].astype(BF16)

    @pl.when(jnp.logical_and(n >= 8, n < WP_NBLK - 1))
    def _():
        o_ref[:keep] = main_ref[WP_SHIFT:].astype(BF16)
        o_ref[keep:] = extra_ref[...].astype(BF16)

    @pl.when(n == WP_NBLK - 1)
    def _():
        o_ref[:WP_SHIFT] = main_ref[:WP_SHIFT].astype(BF16)
        o_ref[WP_SHIFT:] = jnp.zeros((keep, o_ref.shape[1]), BF16)


def _wprep_call(w_t):
    d = w_t.shape[1]
    ratio = WP_ROWS // WP_SHIFT
    return pl.pallas_call(
        _wprep_kernel,
        grid=(WP_NBLK,),
        in_specs=[pl.BlockSpec((WP_ROWS, d), lambda n: (_wprep_src(n), 0)),
                  pl.BlockSpec((WP_SHIFT, d), lambda n: (ratio * _wprep_src(n) + ratio, 0))],
        out_specs=pl.BlockSpec((WP_ROWS, d), lambda n: (n, 0)),
        out_shape=jax.ShapeDtypeStruct((P_COLS, d), BF16),
        compiler_params=_params(1),
        name="wprep",
    )(w_t, w_t)


def _mod_kernel(c_ref, w_ref, b_ref, o_ref):
    o_ref[...] = _bdot(_silu(c_ref[...]), w_ref[...]) + b_ref[...]


def _mod_call(cc, w_mod, b_mod):
    tn = 768
    n = w_mod.shape[1]
    return pl.pallas_call(
        _mod_kernel,
        grid=(n // tn,),
        in_specs=[pl.BlockSpec((8, D_MODEL), lambda j: (0, 0)),
                  pl.BlockSpec((D_MODEL, tn), lambda j: (0, j)),
                  pl.BlockSpec((1, tn), lambda j: (0, j))],
        out_specs=pl.BlockSpec((8, tn), lambda j: (0, j)),
        out_shape=jax.ShapeDtypeStruct((8, n), F32),
        compiler_params=_params(1),
        name="mod",
    )(cc, w_mod, b_mod)


def _inproj_kernel(x_ref, shift_ref, scale_ref, nw_ref, w_ref, o_ref, h_ref):
    @pl.when(pl.program_id(2) == 0)
    def _():
        x = x_ref[0]
        r = lax.rsqrt(jnp.mean(x * x, axis=-1, keepdims=True) + EPS)
        gain = nw_ref[...] * (1.0 + scale_ref[0])
        h_ref[...] = ((x * r) * gain + shift_ref[0]).astype(BF16)

    o_ref[0] = _bdot_nt(h_ref[...], w_ref[...]).astype(BF16)


def _inproj_call(x, mod3, norm_w, w_all, *, mod_row, w_col, n_col, tn, name):
    nb, t, d = x.shape
    tm = 1024
    return pl.pallas_call(
        _inproj_kernel,
        grid=(nb, t // tm, n_col),
        in_specs=[pl.BlockSpec((1, tm, d), lambda bi, i, j: (bi, i, 0)),
                  pl.BlockSpec((1, 1, d), lambda bi, i, j: (mod_row(bi), 0, 0)),
                  pl.BlockSpec((1, 1, d), lambda bi, i, j: (mod_row(bi), 0, 1)),
                  pl.BlockSpec((1, d), lambda bi, i, j: (0, 0)),
                  pl.BlockSpec((tn, d), lambda bi, i, j: (w_col(j), 0))],
        out_specs=pl.BlockSpec((1, tm, tn), lambda bi, i, j: (bi, i, j)),
        out_shape=jax.ShapeDtypeStruct((nb, t, n_col * tn), BF16),
        scratch_shapes=[pltpu.VMEM((tm, d), BF16)],
        compiler_params=_params(3, INPROJ_VMEM_LIMIT),
        name=name,
    )(x, mod3, mod3, norm_w, w_all)


PREP_T = 256
HALO = 16


CONV_OFFS = (-2, -1, 1, 2)


def _prep_kernel(*refs, latent):
    if latent:
        (main_ref, prev_ref, next_ref, small_ref, shift_ref, convw_ref, alog_ref, dtb_ref, knw_ref,
         cos_ref, sin_ref, g_ref, bg_ref, kn_ref, vx_ref) = refs
        i = pl.program_id(1)
        has_prev = i > 0
        has_next = i < pl.num_programs(1) - 1
        row8 = lax.broadcasted_iota(jnp.int32, (8, 1024), 0)
    else:
        (main_ref, small_ref, shift_ref, convw_ref, alog_ref, dtb_ref, knw_ref,
         g_ref, bg_ref, kn_ref, vx_ref) = refs
    t = PREP_T
    for c in range(3):
        cs = slice(c * 1024, (c + 1) * 1024)
        w = [convw_ref[k:k + 1, cs] for k in range(A_CONV)]
        x = main_ref[0, :, cs]
        xw = jnp.concatenate([x * w[k].astype(BF16) for k in (0, 1, 3, 4)], axis=0)
        acc = x.astype(F32) * w[2] + jnp.dot(shift_ref[...], xw, preferred_element_type=F32)
        if latent:
            pv = jnp.where(has_prev, prev_ref[0, HALO - 8:, cs].astype(F32), 0.0)
            nx = jnp.where(has_next, next_ref[0, :8, cs].astype(F32), 0.0)
            top = (jnp.where(row8 < 2, pltpu.roll(pv, 2, 0), 0.0) * w[0]
                   + jnp.where(row8 < 1, pltpu.roll(pv, 1, 0), 0.0) * w[1])
            bot = (jnp.where(row8 >= 7, pltpu.roll(nx, 7, 0), 0.0) * w[3]
                   + jnp.where(row8 >= 6, pltpu.roll(nx, 6, 0), 0.0) * w[4])
            acc = jnp.concatenate([acc[:8] + top, acc[8:t - 8], acc[t - 8:] + bot], axis=0)
        act = _silu(acc)
        if c == 2:
            g_ref[0, :, cs] = act.astype(BF16)
        else:
            mul = HD ** -0.5 if c == 0 else 1.0
            for h in range(HEADS):
                a = act[:, h * HD:(h + 1) * HD]
                r = lax.rsqrt(jnp.sum(a * a, axis=-1, keepdims=True) + EPS)
                g_ref[0, :, c * 1024 + h * HD:c * 1024 + (h + 1) * HD] = (a * (r * mul)).astype(BF16)

    ba = small_ref[0, :, SMALL_BA:SMALL_BA + LANES].astype(F32)
    beta = jax.nn.sigmoid(ba)
    gdec = -jnp.exp(alog_ref[...]) * _softplus(ba + dtb_ref[...])
    lane = lax.broadcasted_iota(jnp.int32, ba.shape, 1)
    bg_ref[0] = jnp.where(lane < 2 * HEADS, beta, gdec)

    for h in range(KV_HEADS):
        kh = small_ref[0, :, SMALL_BK + h * HD:SMALL_BK + (h + 1) * HD].astype(F32)
        r = lax.rsqrt(jnp.mean(kh * kh, axis=-1, keepdims=True) + EPS)
        kn = kh * r * knw_ref[...]
        if latent:
            kn = kn * cos_ref[...] + pltpu.roll(kn, HD // 2, 1) * sin_ref[...]
        kn_ref[0, :, h * HD:(h + 1) * HD] = kn.astype(BF16)
        vx_ref[0, :, 2 * h * HD:(2 * h + 1) * HD] = small_ref[0, :, SMALL_BV + h * HD:SMALL_BV + (h + 1) * HD]
        vx_ref[0, :, (2 * h + 1) * HD:(2 * h + 2) * HD] = jnp.ones((PREP_T, HD), BF16)


def _prep_call(p, conv_w, alog_row, dtb_row, k_norm_w, cosf, sinf, *, latent):
    b, t, _ = p.shape
    nt = t // PREP_T
    const = lambda bi, i: (0, 0)
    tile = lambda bi, i: (bi, i, 0)
    if latent:
        hb = PREP_T // HALO
        last_hb = t // HALO - 1
        small_col = COL_SMALL // 1024
        in_specs = [pl.BlockSpec((1, PREP_T, 3072), tile),
                    pl.BlockSpec((1, HALO, 3072), lambda bi, i: (bi, jnp.maximum(i * hb - 1, 0), 0)),
                    pl.BlockSpec((1, HALO, 3072), lambda bi, i: (bi, jnp.minimum((i + 1) * hb, last_hb), 0)),
                    pl.BlockSpec((1, PREP_T, 1024), lambda bi, i: (bi, i, small_col))]
        args = [p, p, p, p]
    else:
        small_col = CTX_COL_SMALL // 1024
        in_specs = [pl.BlockSpec((1, PREP_T, 3072), tile),
                    pl.BlockSpec((1, PREP_T, 1024), lambda bi, i: (bi, i, small_col))]
        args = [p, p]
    shifts = jnp.concatenate([jnp.eye(PREP_T, k=off, dtype=BF16) for off in CONV_OFFS], axis=1)
    in_specs += [pl.BlockSpec(shifts.shape, const),
                 pl.BlockSpec((A_CONV, 3072), const), pl.BlockSpec((1, LANES), const),
                 pl.BlockSpec((1, LANES), const), pl.BlockSpec((1, HD), const)]
    args += [shifts, conv_w, alog_row, dtb_row, k_norm_w]
    if latent:
        in_specs += [pl.BlockSpec((PREP_T, HD), lambda bi, i: (i, 0))] * 2
        args += [cosf, sinf]
    return pl.pallas_call(
        functools.partial(_prep_kernel, latent=latent),
        grid=(b, nt),
        in_specs=in_specs,
        out_specs=[pl.BlockSpec((1, PREP_T, 3072), tile),
                   pl.BlockSpec((1, PREP_T, LANES), tile),
                   pl.BlockSpec((1, PREP_T, KV_HEADS * HD), tile),
                   pl.BlockSpec((1, PREP_T, 2 * KV_HEADS * HD), tile)],
        out_shape=[jax.ShapeDtypeStruct((b, t, 3072), BF16),
                   jax.ShapeDtypeStruct((b, t, LANES), F32),
                   jax.ShapeDtypeStruct((b, t, KV_HEADS * HD), BF16),
                   jax.ShapeDtypeStruct((b, t, 2 * KV_HEADS * HD), BF16)],
        compiler_params=_params(2),
        name="prep_latent" if latent else "prep_ctx",
    )(*args)


def _chunk_masks(direction):
    c = CHUNK
    ii = lax.broadcasted_iota(jnp.int32, (c, c), 0)
    jj = lax.broadcasted_iota(jnp.int32, (c, c), 1)
    if direction == 0:
        return ii >= jj, ii > jj, c - 1
    return ii <= jj, ii < jj, 0


def _chunk_cumsum(bg, incl):
    tri = incl.astype(BF16)
    g1 = bg.astype(BF16)
    r1 = bg - g1.astype(F32)
    g2 = r1.astype(BF16)
    g3 = (r1 - g2.astype(F32)).astype(BF16)
    return (jnp.dot(tri, g1, preferred_element_type=F32) + jnp.dot(tri, g2, preferred_element_type=F32)
            + jnp.dot(tri, g3, preferred_element_type=F32))


def _gdn_kernel(*refs, latent):
    step = pl.program_id(1)
    if latent:
        qkv_f_ref, bg_f_ref, qkv_b_ref, bg_b_ref, s0_f_ref, s0_b_ref, o_f_ref, o_b_ref, s_f_ref, s_b_ref = refs

        @pl.when(step == 0)
        def _():
            s_f_ref[...] = s0_f_ref[...]
            s_b_ref[...] = s0_b_ref[...]

        o_refs = (o_f_ref, o_b_ref)
    else:
        qkv_f_ref, bg_f_ref, qkv_b_ref, bg_b_ref, s_f_ref, s_b_ref = refs

        @pl.when(step == 0)
        def _():
            s_f_ref[...] = jnp.zeros(s_f_ref.shape, F32)
            s_b_ref[...] = jnp.zeros(s_b_ref.shape, F32)

    state = lambda n, d, h: (s_f_ref, s_b_ref)[d].at[n, h]
    c = CHUNK
    qkv_refs = (qkv_f_ref, qkv_b_ref)
    rows = range(qkv_f_ref.shape[0])
    chains = [(n, d, h) for n in rows for d in range(2) for h in range(HEADS)]

    masks = [_chunk_masks(d) for d in range(2)]
    bg_refs = (bg_f_ref, bg_b_ref)
    bgs = {(n, d): bg_refs[d][n] for n in rows for d in range(2)}
    gcs = {nd: _chunk_cumsum(bgs[nd], masks[nd[1]][0]) for nd in bgs}
    gcts = {nd: jnp.concatenate([gcs[nd], jnp.zeros((LANES - c, LANES), F32)], axis=0).T for nd in bgs}

    q, k, v, kf, beta, gcol, glast, decay = {}, {}, {}, {}, {}, {}, {}, {}
    for ch in chains:
        n, d, h = ch
        incl, _, last = masks[d]
        cb = d * HEADS + h
        cg = 2 * HEADS + cb
        q[ch] = qkv_refs[d][n, :, h * HD:(h + 1) * HD]
        k[ch] = qkv_refs[d][n, :, 1024 + h * HD:1024 + (h + 1) * HD]
        v[ch] = qkv_refs[d][n, :, 2048 + h * HD:2048 + (h + 1) * HD]
        kf[ch] = k[ch].astype(F32)
        beta[ch] = bgs[n, d][:, cb:cb + 1]
        gcol[ch] = gcs[n, d][:, cg:cg + 1]
        grow = gcts[n, d][cg:cg + 1, 0:c]
        glast[ch] = gcs[n, d][last:last + 1, cg:cg + 1]
        decay[ch] = jnp.where(incl, jnp.exp(jnp.where(incl, gcol[ch] - grow, 0.0)), 0.0)

    if latent:
        qk = {ch: _bdot_nt(q[ch], k[ch]) for ch in chains}
        attn = {ch: qk[ch] * decay[ch] for ch in chains}

    gp = GDN_PACK
    groups = [(n, d, hg) for n in rows for d in range(2) for hg in range(HEADS // gp)]
    members = {gr: [(gr[0], gr[1], gr[2] * gp + m) for m in range(gp)] for gr in groups}
    lane_c = lax.broadcasted_iota(jnp.int32, (c, gp * c), 1)
    row_c = lax.broadcasted_iota(jnp.int32, (c, gp * c), 0)
    eye4 = ((lane_c & (c - 1)) == row_c).astype(F32)
    blk_c = [((lane_c >> (c.bit_length() - 1)) == m).astype(BF16) for m in range(gp)]
    lane_k = lax.broadcasted_iota(jnp.int32, (c, gp * HD), 1)
    blk_k = [((lane_k >> (HD.bit_length() - 1)) == m).astype(BF16) for m in range(gp)]

    def bd(x4):
        xb = x4.astype(BF16)
        return jnp.concatenate([xb * blk_c[m] for m in range(gp)], axis=0)

    def pdot(a, b):
        return jnp.dot(a.astype(BF16), b, preferred_element_type=F32)

    k4 = {(n, d, hg): qkv_refs[d][n, :, 1024 + hg * gp * HD:1024 + (hg + 1) * gp * HD]
          for n, d, hg in groups}
    kk4 = {gr: _bdot_nt(k4[gr], jnp.concatenate([k4[gr] * blk_k[m] for m in range(gp)], axis=0))
           for gr in groups}
    l4 = {gr: kk4[gr] * jnp.concatenate(
        [beta[ch] * jnp.where(masks[ch[1]][1], decay[ch], 0.0) for ch in members[gr]], axis=1)
        for gr in groups}

    t4 = {gr: eye4 - l4[gr] for gr in groups}
    lp4 = {gr: pdot(l4[gr], bd(l4[gr])) for gr in groups}
    n_sq = c.bit_length() - 2
    for lvl in range(n_sq):
        last_lvl = lvl == n_sq - 1
        lhs4 = {gr: t4[gr] if last_lvl else jnp.concatenate([t4[gr], lp4[gr]], axis=0) for gr in groups}
        res = {gr: pdot(lhs4[gr], bd(lp4[gr])) for gr in groups}
        t4 = {gr: t4[gr] + res[gr][:c] for gr in groups}
        if not last_lvl:
            lp4 = {gr: res[gr][c:] for gr in groups}

    t0 = {gr: t4[gr].astype(BF16) for gr in groups}
    a4 = {gr: eye4 + l4[gr] for gr in groups}
    a_hi = {gr: a4[gr].astype(BF16) for gr in groups}
    a_lo = {gr: (a4[gr] - a_hi[gr].astype(F32)).astype(BF16) for gr in groups}
    bd_t0 = {gr: bd(t0[gr]) for gr in groups}
    at = {gr: pdot(jnp.concatenate([a_hi[gr], a_lo[gr]], axis=0), bd_t0[gr]) for gr in groups}
    resid = {gr: eye4 - (at[gr][:c] + at[gr][c:]) for gr in groups}
    t1 = {gr: t0[gr].astype(F32) + pdot(t0[gr], bd(resid[gr])) for gr in groups}

    eg = {ch: jnp.exp(gcol[ch]) for ch in chains}
    rhs = {ch: jnp.concatenate([v[ch].astype(F32) * beta[ch], kf[ch] * (beta[ch] * eg[ch])], axis=1)
           for ch in chains}
    rhs4 = {gr: jnp.concatenate([rhs[ch] for ch in members[gr]], axis=0).astype(BF16) for gr in groups}
    uw4 = {gr: jnp.dot(bd(t1[gr]), rhs4[gr], preferred_element_type=F32) for gr in groups}
    uw = {ch: uw4[gr][m * c:(m + 1) * c] for gr in groups for m, ch in enumerate(members[gr])}
    s = {ch: state(*ch)[...] for ch in chains}
    if latent:
        lhs = {ch: jnp.concatenate([uw[ch][:, HD:], q[ch].astype(F32) * eg[ch]], axis=0) for ch in chains}
    else:
        lhs = {ch: uw[ch][:, HD:] for ch in chains}
    ws_qs = {ch: _bdot(lhs[ch], s[ch]) for ch in chains}
    v_new = {ch: uw[ch][:, :HD] - ws_qs[ch][:c] for ch in chains}
    k_dec = {ch: kf[ch] * jnp.exp(glast[ch] - gcol[ch]) for ch in chains}
    kv = {ch: _bdot_tn(k_dec[ch], v_new[ch]) for ch in chains}
    if latent:
        av = {ch: _bdot(attn[ch], v_new[ch]) for ch in chains}
    for ch in chains:
        n, d, h = ch
        if latent:
            o_refs[d][n, :, h * HD:(h + 1) * HD] = (ws_qs[ch][c:] + av[ch]).astype(BF16)
        state(*ch)[...] = s[ch] * jnp.exp(glast[ch]) + kv[ch]


GDN_ROWS = 4
GDN_PACK = 4


def _gdn_call(g, bg, states=None):
    b, t, _ = g.shape
    nc = t // CHUNK
    nr = GDN_ROWS
    latent = states is not None
    fwd = lambda bi, s: (bi, s, 0)
    bwd = lambda bi, s: (bi, nc - 1 - s, 0)
    in_specs = [pl.BlockSpec((nr, CHUNK, 3072), fwd), pl.BlockSpec((nr, CHUNK, LANES), fwd),
                pl.BlockSpec((nr, CHUNK, 3072), bwd), pl.BlockSpec((nr, CHUNK, LANES), bwd)]
    state_spec = pl.BlockSpec((nr, HEADS, HD, HD), lambda bi, s: (bi, 0, 0, 0))
    state_shape = jax.ShapeDtypeStruct((b, HEADS, HD, HD), F32)
    if latent:
        return pl.pallas_call(
            functools.partial(_gdn_kernel, latent=True),
            grid=(b // nr, nc),
            in_specs=in_specs + [state_spec, state_spec],
            out_specs=[pl.BlockSpec((nr, CHUNK, HEADS * HD), fwd),
                       pl.BlockSpec((nr, CHUNK, HEADS * HD), bwd)],
            out_shape=[jax.ShapeDtypeStruct((b, t, HEADS * HD), BF16)] * 2,
            scratch_shapes=[pltpu.VMEM((nr, HEADS, HD, HD), F32)] * 2,
            compiler_params=_params(2),
            name="gdn_latent",
        )(g, bg, g, bg, *states)
    return pl.pallas_call(
        functools.partial(_gdn_kernel, latent=False),
        grid=(b // nr, nc),
        in_specs=in_specs,
        out_specs=[state_spec, state_spec],
        out_shape=[state_shape, state_shape],
        compiler_params=_params(2),
        name="gdn_ctx",
    )(g, bg, g, bg)


ATT_TQ = 512
Q_PER_KV = HEADS // KV_HEADS
Q_SCALE = HD ** -0.5 * math.log2(math.e)


def _attn_kernel(q_ref, z_ref, kl_ref, kc_ref, vl_ref, vc_ref, qnw_ref, cos_ref, sin_ref, o_ref):
    def scores(j):
        g = slice((j // Q_PER_KV) * HD, (j // Q_PER_KV + 1) * HD)
        qh = q_ref[0, :, j * HD:(j + 1) * HD].astype(F32)
        r = lax.rsqrt(jnp.mean(qh * qh, axis=-1, keepdims=True) + EPS)
        qn = qh * r * qnw_ref[...]
        qr = (qn * cos_ref[...] + pltpu.roll(qn, HD // 2, 1) * sin_ref[...]) * Q_SCALE
        return _bdot_nt(qr, kl_ref[0, :, g]), _bdot_nt(qr, kc_ref[0, :, g])

    nxt = scores(0)
    for j in range(HEADS):
        sl = slice(j * HD, (j + 1) * HD)
        gx = slice((j // Q_PER_KV) * 2 * HD, (j // Q_PER_KV + 1) * 2 * HD)
        s_l, s_c = nxt
        if j + 1 < HEADS:
            nxt = scores(j + 1)
        m = jnp.maximum(jnp.max(s_l, axis=-1, keepdims=True), jnp.max(s_c, axis=-1, keepdims=True))
        e_l = jnp.exp2(s_l - m).astype(BF16)
        e_c = jnp.exp2(s_c - m).astype(BF16)
        ox = (jnp.dot(e_l, vl_ref[0, :, gx], preferred_element_type=F32)
              + jnp.dot(e_c, vc_ref[0, :, gx], preferred_element_type=F32))
        o = ox[:, :HD] / ox[:, HD:]
        o_ref[0, :, sl] = (o * _silu(z_ref[0, :, sl].astype(F32))).astype(BF16)


def _attn_call(p, kn, kn_ctx, vx, vx_ctx, q_norm_w, cosf, sinf):
    b = p.shape[0]
    qw = HEADS * HD
    kvw = KV_HEADS * HD
    return pl.pallas_call(
        _attn_kernel,
        grid=(b, SEQ // ATT_TQ),
        in_specs=[pl.BlockSpec((1, ATT_TQ, qw), lambda bi, i: (bi, i, COL_BQ // qw)),
                  pl.BlockSpec((1, ATT_TQ, qw), lambda bi, i: (bi, i, COL_BZ // qw)),
                  pl.BlockSpec((1, SEQ, kvw), lambda bi, i: (bi, 0, 0)),
                  pl.BlockSpec((1, CTX_LEN, kvw), lambda bi, i: (bi, 0, 0)),
                  pl.BlockSpec((1, SEQ, 2 * kvw), lambda bi, i: (bi, 0, 0)),
                  pl.BlockSpec((1, CTX_LEN, 2 * kvw), lambda bi, i: (bi, 0, 0)),
                  pl.BlockSpec((1, HD), lambda bi, i: (0, 0)),
                  pl.BlockSpec((ATT_TQ, HD), lambda bi, i: (i, 0)),
                  pl.BlockSpec((ATT_TQ, HD), lambda bi, i: (i, 0))],
        out_specs=pl.BlockSpec((1, ATT_TQ, qw), lambda bi, i: (bi, i, 0)),
        out_shape=jax.ShapeDtypeStruct((b, SEQ, qw), BF16),
        compiler_params=_params(2),
        name="attn",
    )(p, p, kn, kn_ctx, vx, vx_ctx, q_norm_w, cosf, sinf)


MERGE_T = 512


def _merge_kernel(of_ref, ob_ref, az_ref, bl_ref, bra_ref, brb_ref, x_ref, gate_ref, anw_ref,
                  wpa_ref, wpb_ref, wo_ref, out_ref, a_ref):
    o = of_ref[0].astype(F32) + ob_ref[0].astype(F32)
    for h in range(HEADS):
        sl = slice(h * HD, (h + 1) * HD)
        oh = o[:, sl]
        r = lax.rsqrt(jnp.mean(oh * oh, axis=-1, keepdims=True) + EPS)
        a_ref[:, sl] = (oh * r * anw_ref[...] * _silu(az_ref[0, :, sl].astype(F32))).astype(BF16)
    pa = jnp.dot(a_ref[...], wpa_ref[...], preferred_element_type=F32)
    pb = jnp.dot(bl_ref[0], wpb_ref[...], preferred_element_type=F32)
    m = (jax.nn.sigmoid(bra_ref[0].astype(F32)) * pa + jax.nn.sigmoid(brb_ref[0].astype(F32)) * pb)
    y = jnp.dot(m.astype(BF16), wo_ref[...], preferred_element_type=F32)
    out_ref[0] = x_ref[0] + gate_ref[0] * y


def _merge_call(o_f, o_b, p, b_lat, x, mod3, a_norm_w, wpa, wpb, wo):
    b, t, d = x.shape
    tm = MERGE_T
    tile = lambda bi, i: (bi, i, 0)
    const = lambda bi, i: (0, 0)
    return pl.pallas_call(
        _merge_kernel,
        grid=(b, t // tm),
        in_specs=[pl.BlockSpec((1, tm, 1024), tile),
                  pl.BlockSpec((1, tm, 1024), tile),
                  pl.BlockSpec((1, tm, 1024), lambda bi, i: (bi, i, COL_AZ // 1024)),
                  pl.BlockSpec((1, tm, 1024), tile),
                  pl.BlockSpec((1, tm, d), lambda bi, i: (bi, i, COL_BRA // d)),
                  pl.BlockSpec((1, tm, d), lambda bi, i: (bi, i, COL_BRB // d)),
                  pl.BlockSpec((1, tm, d), tile),
                  pl.BlockSpec((1, 1, d), lambda bi, i: (bi, 0, 2)),
                  pl.BlockSpec((1, HD), const),
                  pl.BlockSpec((1024, d), const, pipeline_mode=pl.Buffered(1)),
                  pl.BlockSpec((1024, d), const, pipeline_mode=pl.Buffered(1)),
                  pl.BlockSpec((d, d), const, pipeline_mode=pl.Buffered(1))],
        out_specs=pl.BlockSpec((1, tm, d), tile),
        out_shape=jax.ShapeDtypeStruct((b, t, d), F32),
        scratch_shapes=[pltpu.VMEM((tm, 1024), BF16)],
        compiler_params=_params(2),
        name="merge",
    )(o_f, o_b, p, b_lat, p, p, x, mod3, a_norm_w, wpa, wpb, wo)


def _rope_tables():
    t = jnp.arange(SEQ)
    row = (t // GRID_W).astype(F32)
    col = (t % GRID_W).astype(F32)
    n_freq = HD // 4
    inv = ROPE_THETA ** (-jnp.arange(n_freq, dtype=F32) / n_freq)
    ang = jnp.concatenate([row[:, None] * inv, col[:, None] * inv], axis=-1)
    cos, sin = jnp.cos(ang), jnp.sin(ang)
    return jnp.concatenate([cos, cos], axis=-1), jnp.concatenate([-sin, sin], axis=-1)


def _gate_row(a):
    return jnp.zeros((1, LANES), F32).at[0, 2 * HEADS:4 * HEADS].set(a.reshape(-1).astype(F32))


def _layer(x, ctx, c, c_ctx, w_mod, b_mod, norm_w, w_in_t, conv_w, a_log, dt_bias,
           a_norm_w, q_norm_w, k_norm_w, w_proj_a, w_proj_b, w_out):
    b = x.shape[0]
    d = D_MODEL
    w_all = _wprep_call(w_in_t)
    wpa = w_proj_a.astype(BF16)
    wpb = w_proj_b.astype(BF16)
    wo = w_out.astype(BF16)

    cc = jnp.concatenate([c, c_ctx[None, :], jnp.zeros((8 - b - 1, d), F32)], axis=0)
    mod3 = _mod_call(cc, w_mod, b_mod[None, :]).reshape(8, 1, 3 * d)

    nw = norm_w[None, :]
    p = _inproj_call(x, mod3, nw, w_all, mod_row=lambda bi: bi, w_col=lambda j: j,
                     n_col=P_COLS // INPROJ_TN, tn=INPROJ_TN, name="inproj_latent")
    small_blk = COL_SMALL // 1024
    p_ctx = _inproj_call(ctx.reshape(1, b * CTX_LEN, d), mod3, nw, w_all, mod_row=lambda bi: b,
                         w_col=lambda j: jnp.where(j < 3, j, small_blk), n_col=CTX_COLS // 1024,
                         tn=1024, name="inproj_ctx").reshape(b, CTX_LEN, CTX_COLS)

    cosf, sinf = _rope_tables()
    alog_row, dtb_row, knw = _gate_row(a_log), _gate_row(dt_bias), k_norm_w[None, :]
    g, bg, kn, vx = _prep_call(p, conv_w, alog_row, dtb_row, knw, cosf, sinf, latent=True)
    g_ctx, bg_ctx, kn_ctx, vx_ctx = _prep_call(p_ctx, conv_w, alog_row, dtb_row, knw, None, None,
                                               latent=False)
    states = _gdn_call(g_ctx, bg_ctx)
    o_f, o_b = _gdn_call(g, bg, states)
    b_lat = _attn_call(p, kn, kn_ctx, vx, vx_ctx, q_norm_w[None, :], cosf, sinf)
    return _merge_call(o_f, o_b, p, b_lat, x, mod3, a_norm_w[None, :], wpa, wpb, wo)


def kernel(x, c, ctx, c_ctx, w_mod, b_mod, norm_w, w_in, conv_w, a_log, dt_bias, a_norm_w,
           q_norm_w, k_norm_w, w_proj_a, w_proj_b, w_out):
    assert w_mod.shape[0] == 1
    w_in_t = jnp.swapaxes(w_in, 1, 2).reshape(w_in.shape[2], w_in.shape[1])
    return _layer(x, ctx, c, c_ctx, w_mod[0], b_mod[0], norm_w[0], w_in_t, conv_w[0], a_log[0],
                  dt_bias[0], a_norm_w[0], q_norm_w[0], k_norm_w[0], w_proj_a[0], w_proj_b[0],
                  w_out[0])
```
